```python
import math
import jax, jax.numpy as jnp
from jax import lax
import numpy as np

D_MODEL = 1024
BATCH = 32
SEQ = 256
DEPTH = 2
DEC_BATCH = 4
DEC_SEQ = 4096
PAST_LEN = 512

GRID_W = 64
EPS = 1e-6
ROPE_BASE = 10000.0
Q_BLOCK = 128
SSD_HEAD_DIM = 64
SSD_INNER = D_MODEL // 2
SSD_HEADS = SSD_INNER // SSD_HEAD_DIM
SSD_GROUPS = 2
SSD_STATE = 128
SSD_CONV = 3
SSD_CHUNK = 128
SSD_GN = SSD_GROUPS * SSD_STATE
SSD_CONV_CH = SSD_INNER + 2 * SSD_GN
SSD_COLS = SSD_INNER + SSD_CONV_CH + 2 * SSD_HEADS
MLA_V = 64
MLA_HEADS = (D_MODEL // 4) // MLA_V
MLA_NOPE = 64
MLA_ROPE = 32
MLA_Q_LORA = 256
MLA_KV_LORA = 128
MLA_COLS = MLA_Q_LORA + MLA_KV_LORA + MLA_ROPE
GQA_HEAD_DIM = 64
GQA_HEADS = (D_MODEL // 4) // GQA_HEAD_DIM
GQA_KV_HEADS = 2
GQA_COLS = (GQA_HEADS + 2 * GQA_KV_HEADS) * GQA_HEAD_DIM
IN_COLS = SSD_COLS + MLA_COLS + GQA_COLS
MIX_W = SSD_INNER + MLA_HEADS * MLA_V + GQA_HEADS * GQA_HEAD_DIM
D_FF = 2816
N_EXPERTS = 8
TOP_K = 2
MOE_FF = 1792
N_DENSE = (DEPTH + 1) // 2
N_MOE = DEPTH // 2

kernel_name = 'hybrid_ssd_mla_gqa_diffusion_step'

F32 = jnp.float32


def rmsnorm(x, g):
    xf = x.astype(F32)
    y = xf * lax.rsqrt(jnp.mean(xf * xf, axis=-1, keepdims=True) + EPS)
    return (y * g.astype(F32)).astype(x.dtype)


def rope_1d(x, pos):
    d = x.shape[-1]
    inv = ROPE_BASE ** (-jnp.arange(0, d, 2, dtype=F32) / d)
    ang = pos.astype(F32)[:, None] * inv[None, :]
    cos = jnp.cos(ang)[None, :, None, :]
    sin = jnp.sin(ang)[None, :, None, :]
    xf = x.astype(F32)
    x1, x2 = xf[..., : d // 2], xf[..., d // 2:]
    return jnp.concatenate([x1 * cos - x2 * sin, x2 * cos + x1 * sin], axis=-1).astype(x.dtype)


def rope_2d(x, row, col):
    half = x.shape[-1] // 2
    return jnp.concatenate([rope_1d(x[..., :half], row), rope_1d(x[..., half:], col)], axis=-1)


def grid_positions(n_tokens):
    rows = n_tokens // GRID_W
    row = jnp.repeat(jnp.arange(rows, dtype=jnp.int32), GRID_W)
    col = jnp.tile(jnp.arange(GRID_W, dtype=jnp.int32), rows)
    return row, col


def blocked_attention(q, k, v, scale):
    b, lq, h, d = q.shape
    hkv, dv = k.shape[2], v.shape[-1]
    g = h // hkv
    nb = lq // Q_BLOCK
    qb = q.reshape(b, nb, Q_BLOCK, hkv, g, d).transpose(1, 0, 2, 3, 4, 5)

    def one_block(qblk):
        s = jnp.einsum('bqkgd,bskd->bkgqs', qblk, k, preferred_element_type=F32) * scale
        pr = jax.nn.softmax(s, axis=-1).astype(v.dtype)
        return jnp.einsum('bkgqs,bskd->bqkgd', pr, v)

    o = lax.map(one_block, qb)
    return o.transpose(1, 0, 2, 3, 4, 5).reshape(b, lq, h, dv)


def dwconv_centred(u, w, bias):
    y = lax.conv_general_dilated(
        u, w[:, None, :], window_strides=(1,), padding=[(SSD_CONV // 2, SSD_CONV // 2)],
        dimension_numbers=('NWC', 'WIO', 'NWC'), feature_group_count=u.shape[-1])
    return y + bias


def ssd_chunked(x, dt, A, B, C, h0):
    b, L, H, P = x.shape
    N = B.shape[-1]
    nc = L // SSD_CHUNK
    Q = SSD_CHUNK
    x = x.reshape(b, nc, Q, H, P)
    dt = dt.reshape(b, nc, Q, H)
    B = B.reshape(b, nc, Q, H, N)
    C = C.reshape(b, nc, Q, H, N)
    a_cum = jnp.cumsum(dt * A, axis=2)
    xdt = x * dt[..., None]
    lower = jnp.tril(jnp.ones((Q, Q), dtype=bool))[None, None, :, :, None]
    seg = a_cum[:, :, :, None, :] - a_cum[:, :, None, :, :]
    decay = jnp.exp(jnp.where(lower, seg, -jnp.inf))
    scores = jnp.einsum('bcqhn,bcshn->bcqsh', C, B) * decay
    y_diag = jnp.einsum('bcqsh,bcshp->bcqhp', scores, xdt)
    decay_to_end = jnp.exp(a_cum[:, :, -1:, :] - a_cum)
    chunk_states = jnp.einsum('bcshn,bcsh,bcshp->bchpn', B, decay_to_end, xdt)
    chunk_decay = jnp.exp(a_cum[:, :, -1, :])

    def step(h, inp):
        s_c, d_c = inp
        return h * d_c[:, :, None, None] + s_c, h

    h_final, h_in = lax.scan(step, h0, (chunk_states.transpose(1, 0, 2, 3, 4), chunk_decay.transpose(1, 0, 2)))
    h_in = h_in.transpose(1, 0, 2, 3, 4)
    y_off = jnp.einsum('bcqhn,bchpn,bcqh->bcqhp', C, h_in, jnp.exp(a_cum))
    return (y_diag + y_off).reshape(b, L, H, P), h_final


def ssd_mixer(u, i, p, h0):
    b, L, _ = u.shape
    z = u[..., :SSD_INNER]
    xbc = u[..., SSD_INNER:SSD_INNER + SSD_CONV_CH]
    dt_raw = u[..., SSD_INNER + SSD_CONV_CH:].reshape(b, L, 2, SSD_HEADS)
    xbc = jax.nn.silu(dwconv_centred(xbc, p['ssd_conv_w'][i], p['ssd_conv_b'][i]))
    hpg = SSD_HEADS // SSD_GROUPS
    xs = xbc[..., :SSD_INNER].reshape(b, L, SSD_HEADS, SSD_HEAD_DIM).astype(F32)
    Bm = jnp.repeat(xbc[..., SSD_INNER:SSD_INNER + SSD_GN].reshape(b, L, SSD_GROUPS, SSD_STATE), hpg, axis=2).astype(F32)
    Cm = jnp.repeat(xbc[..., SSD_INNER + SSD_GN:].reshape(b, L, SSD_GROUPS, SSD_STATE), hpg, axis=2).astype(F32)
    dt = jax.nn.softplus(dt_raw.astype(F32) + p['ssd_dt_bias'][i].astype(F32))
    A = -jnp.exp(p['ssd_A_log'][i].astype(F32))
    h0 = h0.astype(F32)
    flip = lambda t: jnp.flip(t, axis=1)
    y_f, s_f = ssd_chunked(xs, dt[:, :, 0], A[0], Bm, Cm, h0[:, 0])
    y_b, s_b = ssd_chunked(flip(xs), flip(dt[:, :, 1]), A[1], flip(Bm), flip(Cm), h0[:, 1])
    y = y_f + flip(y_b) + xs * p['ssd_D'][i].astype(F32)[:, None]
    y = y.reshape(b, L, SSD_INNER) * jax.nn.silu(z.astype(F32))
    y = rmsnorm(y, p['ssd_norm_g'][i]).astype(u.dtype)
    return y, jnp.stack([s_f, s_b], axis=1).astype(u.dtype)


def mla_mixer(u, i, p, pos, ctx):
    b, L, _ = u.shape
    q_lat = u[..., :MLA_Q_LORA]
    kv_lat = u[..., MLA_Q_LORA:MLA_Q_LORA + MLA_KV_LORA]
    k_pe = u[..., MLA_Q_LORA + MLA_KV_LORA:][:, :, None, :]
    q = (rmsnorm(q_lat, p['mla_q_norm_g'][i]) @ p['mla_w_uq'][i]).reshape(b, L, MLA_HEADS, MLA_NOPE + MLA_ROPE)
    ckv = rmsnorm(kv_lat, p['mla_kv_norm_g'][i])
    q_nope, q_pe = q[..., :MLA_NOPE], q[..., MLA_NOPE:]
    if pos is not None:
        q_pe = rope_2d(q_pe, *pos)
        k_pe_r = rope_2d(k_pe, *pos)
    else:
        k_pe_r = k_pe
    ckv_all, kpe_all = ckv, k_pe_r
    if ctx is not None:
        ckv_all = jnp.concatenate([ctx[0], ckv], axis=1)
        kpe_all = jnp.concatenate([ctx[1][:, :, None, :], k_pe_r], axis=1)
    lk = ckv_all.shape[1]
    kv = (ckv_all @ p['mla_w_ukv'][i]).reshape(b, lk, MLA_HEADS, MLA_NOPE + MLA_V)
    k = jnp.concatenate([kv[..., :MLA_NOPE], jnp.broadcast_to(kpe_all, (b, lk, MLA_HEADS, MLA_ROPE))], axis=-1)
    v = kv[..., MLA_NOPE:]
    qf = jnp.concatenate([q_nope, q_pe], axis=-1)
    o = blocked_attention(qf, k, v, 1.0 / math.sqrt(MLA_NOPE + MLA_ROPE))
    return o.reshape(b, L, MLA_HEADS * MLA_V), (ckv, k_pe[:, :, 0, :])


def gqa_mixer(u, i, p, pos, ctx):
    b, L, _ = u.shape
    nq = GQA_HEADS * GQA_HEAD_DIM
    nk = GQA_KV_HEADS * GQA_HEAD_DIM
    q = rmsnorm(u[..., :nq].reshape(b, L, GQA_HEADS, GQA_HEAD_DIM), p['gqa_q_norm_g'][i])
    k = rmsnorm(u[..., nq:nq + nk].reshape(b, L, GQA_KV_HEADS, GQA_HEAD_DIM), p['gqa_k_norm_g'][i])
    v = u[..., nq + nk:].reshape(b, L, GQA_KV_HEADS, GQA_HEAD_DIM)
    k_ctx_out, v_ctx_out = k, v
    if pos is not None:
        q = rope_2d(q, *pos)
        k = rope_2d(k, *pos)
    if ctx is not None:
        k = jnp.concatenate([ctx[0], k], axis=1)
        v = jnp.concatenate([ctx[1], v], axis=1)
    o = blocked_attention(q, k, v, 1.0 / math.sqrt(GQA_HEAD_DIM))
    return o.reshape(b, L, nq), (k_ctx_out, v_ctx_out)


def swiglu(t, wg, wu, wd):
    return (jax.nn.silu(t @ wg) * (t @ wu)) @ wd


def moe_ffn(h, j, p):
    b, L, d = h.shape
    t = h.reshape(b * L, d)
    logits = jnp.dot(t, p['moe_w_router'][j], preferred_element_type=F32)
    top_v, top_i = lax.top_k(logits, TOP_K)
    gates = jax.nn.softmax(top_v, axis=-1)
    combine = jnp.sum(jax.nn.one_hot(top_i, N_EXPERTS, dtype=F32) * gates[..., None], axis=1)
    out = jnp.zeros((b * L, d), F32)
    for e in range(N_EXPERTS):
        y_e = swiglu(t, p['moe_w_gate'][j, e], p['moe_w_up'][j, e], p['moe_w_down'][j, e]).astype(F32)
        out = out + combine[:, e:e + 1] * y_e
    return out.reshape(b, L, d).astype(h.dtype)


def trunk_layer(i, x, mod, p, pos, ctx):
    sh1, sc1, g1, sh2, sc2, g2 = jnp.split(mod, 6, axis=-1)
    h = rmsnorm(x, p['g_mix_pre'][i]) * (1 + sc1) + sh1
    proj = h @ p['w_in'][i]
    u_ssd = proj[..., :SSD_COLS]
    u_mla = proj[..., SSD_COLS:SSD_COLS + MLA_COLS]
    u_gqa = proj[..., SSD_COLS + MLA_COLS:]
    if ctx is None:
        h0 = jnp.zeros((x.shape[0], 2, SSD_HEADS, SSD_HEAD_DIM, SSD_STATE), x.dtype)
        mla_ctx, gqa_ctx = None, None
    else:
        h0 = ctx[0]
        mla_ctx, gqa_ctx = (ctx[1], ctx[2]), (ctx[3], ctx[4])
    y_ssd, ssd_state = ssd_mixer(u_ssd, i, p, h0)
    y_mla, mla_kv = mla_mixer(u_mla, i, p, pos, mla_ctx)
    y_gqa, gqa_kv = gqa_mixer(u_gqa, i, p, pos, gqa_ctx)
    mix = jnp.concatenate([y_ssd, y_mla, y_gqa], axis=-1) @ p['w_out'][i]
    x = x + g1 * rmsnorm(mix, p['g_mix_post'][i])
    h = rmsnorm(x, p['g_ffn_pre'][i]) * (1 + sc2) + sh2
    if i % 2 == 0:
        j = i // 2
        f = swiglu(h, p['ffn_w_gate'][j], p['ffn_w_up'][j], p['ffn_w_down'][j])
    else:
        f = moe_ffn(h, i // 2, p)
    x = x + g2 * rmsnorm(f, p['g_ffn_post'][i])
    return x, (ssd_state, mla_kv[0], mla_kv[1], gqa_kv[0], gqa_kv[1])


def setup_inputs(seed: int = 0) -> dict:
    key = jax.random.key(seed)
    ks = iter(jax.random.split(key, 64))
    D = D_MODEL

    def nrm(shape, scale):
        return jax.random.normal(next(ks), shape, F32) * scale

    def gain(shape):
        return 1.0 + nrm(shape, 0.02)

    dt0 = jnp.exp(jax.random.uniform(next(ks), (DEPTH, 2, SSD_HEADS), F32, math.log(1e-3), math.log(1e-1)))
    dt_bias = dt0 + jnp.log(-jnp.expm1(-dt0))
    a_log = jnp.log(jax.random.uniform(next(ks), (DEPTH, 2, SSD_HEADS), F32, 1.0, 16.0))
    return {
        'x_prompt': nrm((BATCH, SEQ, D), 1.0),
        'x_sample': nrm((DEC_BATCH, DEC_SEQ, D), 1.0),
        'state_ssd': nrm((DEC_BATCH, DEPTH, 2, SSD_HEADS, SSD_HEAD_DIM, SSD_STATE), 0.5),
        'cache_mla_ckv': nrm((DEC_BATCH, DEPTH, PAST_LEN, MLA_KV_LORA), 1.0),
        'cache_mla_krope': nrm((DEC_BATCH, DEPTH, PAST_LEN, MLA_ROPE), 1.0),
        'cache_gqa_k': nrm((DEC_BATCH, DEPTH, PAST_LEN, GQA_KV_HEADS, GQA_HEAD_DIM), 1.0),
        'cache_gqa_v': nrm((DEC_BATCH, DEPTH, PAST_LEN, GQA_KV_HEADS, GQA_HEAD_DIM), 1.0),
        'c': nrm((DEC_BATCH, D), 1.0),
        'c_ctx': nrm((D,), 1.0),
        'w_mod': nrm((DEPTH, D, 6 * D), 0.5 * D ** -0.5),
        'b_mod': nrm((DEPTH, 6 * D), 0.01),
        'g_mix_pre': gain((DEPTH, D)),
        'g_mix_post': gain((DEPTH, D)),
        'g_ffn_pre': gain((DEPTH, D)),
        'g_ffn_post': gain((DEPTH, D)),
        'w_in': nrm((DEPTH, D, IN_COLS), D ** -0.5),
        'ssd_conv_w': nrm((DEPTH, SSD_CONV, SSD_CONV_CH), SSD_CONV ** -0.5),
        'ssd_conv_b': nrm((DEPTH, SSD_CONV_CH), 0.01),
        'ssd_A_log': a_log,
        'ssd_dt_bias': dt_bias,
        'ssd_D': gain((DEPTH, SSD_HEADS)),
        'ssd_norm_g': gain((DEPTH, SSD_INNER)),
        'mla_q_norm_g': gain((DEPTH, MLA_Q_LORA)),
        'mla_w_uq': nrm((DEPTH, MLA_Q_LORA, MLA_HEADS * (MLA_NOPE + MLA_ROPE)), MLA_Q_LORA ** -0.5),
        'mla_kv_norm_g': gain((DEPTH, MLA_KV_LORA)),
        'mla_w_ukv': nrm((DEPTH, MLA_KV_LORA, MLA_HEADS * (MLA_NOPE + MLA_V)), MLA_KV_LORA ** -0.5),
        'gqa_q_norm_g': gain((DEPTH, GQA_HEAD_DIM)),
        'gqa_k_norm_g': gain((DEPTH, GQA_HEAD_DIM)),
        'w_out': nrm((DEPTH, MIX_W, D), MIX_W ** -0.5),
        'ffn_w_gate': nrm((N_DENSE, D, D_FF), D ** -0.5),
        'ffn_w_up': nrm((N_DENSE, D, D_FF), D ** -0.5),
        'ffn_w_down': nrm((N_DENSE, D_FF, D), D_FF ** -0.5),
        'moe_w_router': nrm((N_MOE, D, N_EXPERTS), D ** -0.5),
        'moe_w_gate': nrm((N_MOE, N_EXPERTS, D, MOE_FF), D ** -0.5),
        'moe_w_up': nrm((N_MOE, N_EXPERTS, D, MOE_FF), D ** -0.5),
        'moe_w_down': nrm((N_MOE, N_EXPERTS, MOE_FF, D), MOE_FF ** -0.5),
    }


def reference(x_prompt, x_sample, state_ssd, cache_mla_ckv, cache_mla_krope, cache_gqa_k, cache_gqa_v,
              c, c_ctx, w_mod, b_mod, g_mix_pre, g_mix_post, g_ffn_pre, g_ffn_post, w_in,
              ssd_conv_w, ssd_conv_b, ssd_A_log, ssd_dt_bias, ssd_D, ssd_norm_g,
              mla_q_norm_g, mla_w_uq, mla_kv_norm_g, mla_w_ukv, gqa_q_norm_g, gqa_k_norm_g, w_out,
              ffn_w_gate, ffn_w_up, ffn_w_down, moe_w_router, moe_w_gate, moe_w_up, moe_w_down):
    p = dict(g_mix_pre=g_mix_pre, g_mix_post=g_mix_post, g_ffn_pre=g_ffn_pre, g_ffn_post=g_ffn_post,
             w_in=w_in, ssd_conv_w=ssd_conv_w, ssd_conv_b=ssd_conv_b, ssd_A_log=ssd_A_log,
             ssd_dt_bias=ssd_dt_bias, ssd_D=ssd_D, ssd_norm_g=ssd_norm_g, mla_q_norm_g=mla_q_norm_g,
             mla_w_uq=mla_w_uq, mla_kv_norm_g=mla_kv_norm_g, mla_w_ukv=mla_w_ukv,
             gqa_q_norm_g=gqa_q_norm_g, gqa_k_norm_g=gqa_k_norm_g, w_out=w_out,
             ffn_w_gate=ffn_w_gate, ffn_w_up=ffn_w_up, ffn_w_down=ffn_w_down,
             moe_w_router=moe_w_router, moe_w_gate=moe_w_gate, moe_w_up=moe_w_up, moe_w_down=moe_w_down)

    y = x_prompt
    collected = [[], [], [], [], []]
    for i in range(DEPTH):
        mod = (jax.nn.silu(c_ctx) @ w_mod[i] + b_mod[i])[None, None, :]
        y, ctx_out = trunk_layer(i, y, mod, p, None, None)
        for lst, t in zip(collected, ctx_out):
            lst.append(t)
    y_prompt = y
    new_state_ssd = jnp.stack(collected[0], axis=1)
    new_cache_mla_ckv = jnp.stack(collected[1], axis=1)
    new_cache_mla_krope = jnp.stack(collected[2], axis=1)
    new_cache_gqa_k = jnp.stack(collected[3], axis=1)
    new_cache_gqa_v = jnp.stack(collected[4], axis=1)

    pos = grid_positions(x_sample.shape[1])
    y = x_sample
    for i in range(DEPTH):
        mod = (jax.nn.silu(c) @ w_mod[i] + b_mod[i])[:, None, :]
        ctx = (state_ssd[:, i], cache_mla_ckv[:, i], cache_mla_krope[:, i], cache_gqa_k[:, i], cache_gqa_v[:, i])
        y, _ = trunk_layer(i, y, mod, p, pos, ctx)
    y_sample = y
    return (y_prompt, y_sample, new_state_ssd, new_cache_mla_ckv, new_cache_mla_krope, new_cache_gqa_k, new_cache_gqa_v)
```

```python
import functools
import math

import numpy as np
import jax
import jax.numpy as jnp
from jax import lax
from jax.experimental import pallas as pl
from jax.experimental.pallas import tpu as pltpu

F32 = jnp.float32
BF16 = jnp.bfloat16

EPS = 1e-6
ROPE_BASE = 10000.0
GRID_W = 64
SSD_HEAD_DIM = 64
SSD_HEADS = 8
SSD_GROUPS = 2
SSD_STATE = 128
SSD_CHUNK = 128
SSD_INNER = SSD_HEADS * SSD_HEAD_DIM
HEADS_PER_GROUP = SSD_HEADS // SSD_GROUPS
MLA_HEADS = 4
MLA_V = 64
MLA_NOPE = 64
MLA_ROPE = 32
MLA_Q_LORA = 256
MLA_KV_LORA = 128
GQA_HEADS = 4
GQA_KV_HEADS = 2
GQA_HEAD_DIM = 64
N_EXPERTS = 8

LANES = 128

COL_Z = 0
COL_X = 512
COL_B = 1024
COL_C = 1280
COL_MLA = 1536
COL_GQA = 2048
COL_DT = 3072
IN_COLS_PADDED = 3328

VMEM_LIMIT = 56 * 1024 * 1024

NT_DIMS = (((1,), (1,)), ((), ()))
TN_DIMS = (((0,), (0,)), ((), ()))


def _cparams(n_grid):
    return pltpu.CompilerParams(
        dimension_semantics=("arbitrary",) * n_grid, vmem_limit_bytes=VMEM_LIMIT)


def _const_spec(shape):
    nd = len(shape)
    return pl.BlockSpec(shape, lambda *_: (0,) * nd, pipeline_mode=pl.Buffered(1))


def _dot(a, b):
    return jnp.dot(a, b, preferred_element_type=F32)


def _rms(x, width=None):
    n = x.shape[-1] if width is None else width
    return x * lax.rsqrt(jnp.sum(x * x, axis=-1, keepdims=True) * (1.0 / n) + EPS)


def _silu(x):
    return x * jax.nn.sigmoid(x)


def _split3(v):
    hi = v.astype(BF16)
    r = v - hi.astype(F32)
    mid = r.astype(BF16)
    lo = (r - mid.astype(F32)).astype(BF16)
    return hi, mid, lo


def _mod_kernel(c_ref, w_ref, b_ref, o_ref):
    s = _silu(c_ref[...]).astype(BF16)
    o_ref[0] = _dot(s, w_ref[0].astype(BF16)) + b_ref[0]


def _modulation(c_all, w_mod, b_mod):
    depth, d, n = w_mod.shape
    tn = 1536
    rows = c_all.shape[0]
    return pl.pallas_call(
        _mod_kernel,
        grid=(depth, n // tn),
        in_specs=[
            pl.BlockSpec((rows, d), lambda i, j: (0, 0)),
            pl.BlockSpec((1, d, tn), lambda i, j: (i, 0, j)),
            pl.BlockSpec((1, 1, tn), lambda i, j: (i, 0, j)),
        ],
        out_specs=pl.BlockSpec((1, rows, tn), lambda i, j: (i, 0, j)),
        out_shape=jax.ShapeDtypeStruct((depth, rows, n), F32),
        compiler_params=_cparams(2),
        name="modulation",
    )(c_all, w_mod, b_mod.reshape(depth, 1, n))


def _inproj_kernel(x_ref, sh_ref, sc_ref, g_ref, w_ref, o_ref):
    h = _rms(x_ref[0]) * g_ref[...]
    h = h * (1.0 + sc_ref[0]) + sh_ref[0]
    o_ref[0] = _dot(h.astype(BF16), w_ref[...])


def _in_projection(x, mod, mod_row, gain, w_in_p, tm):
    b, l, d = x.shape
    n = w_in_p.shape[1]
    return pl.pallas_call(
        _inproj_kernel,
        grid=(b, l // tm),
        in_specs=[
            pl.BlockSpec((1, tm, d), lambda i, t: (i, t, 0)),
            pl.BlockSpec((1, 1, d), lambda i, t: (mod_row(i), 0, 0)),
            pl.BlockSpec((1, 1, d), lambda i, t: (mod_row(i), 0, 1)),
            _const_spec((1, d)),
            _const_spec((d, n)),
        ],
        out_specs=pl.BlockSpec((1, tm, n), lambda i, t: (i, t, 0)),
        out_shape=jax.ShapeDtypeStruct((b, l, n), F32),
        compiler_params=_cparams(2),
        name="in_projection",
    )(x, mod, mod, gain, w_in_p)


def _ssd_kernel(*refs, nc, seq, has_h0, emit_state):
    it = iter(refs)
    x_ref, b_ref, c_ref, dt_ref = next(it), next(it), next(it), next(it)
    cwx_ref, cwb_ref, cwc_ref = next(it), next(it), next(it)
    cbx_ref, cbb_ref, cbc_ref = next(it), next(it), next(it)
    par_ref = next(it)
    h0_ref = next(it) if has_h0 else None
    y_ref = next(it)
    st_ref = next(it) if emit_state else None
    xs_s, bc_s, h_s = next(it), next(it), next(it)

    q = SSD_CHUNK
    hpg = HEADS_PER_GROUP
    wx = hpg * SSD_HEAD_DIM
    row = lax.broadcasted_iota(jnp.int32, (q, 1), 0)
    qi = lax.broadcasted_iota(jnp.int32, (q, q), 0)
    si = lax.broadcasted_iota(jnp.int32, (q, q), 1)
    lower = si <= qi
    upper = si >= qi
    tri = jnp.where(lower, 1.0, 0.0).astype(BF16)
    head_shift = SSD_HEAD_DIM.bit_length() - 1
    lane_blk = lax.broadcasted_iota(jnp.int32, (1, wx), 1) >> head_shift
    row_blk = lax.broadcasted_iota(jnp.int32, (wx, 1), 0) >> head_shift
    par = par_ref[0]
    d_lanes = jnp.zeros((1, wx), F32)
    for hl in range(hpg):
        d_lanes = jnp.where(lane_blk == hl, par[2:3, hl:hl + 1], d_lanes)

    for d in range(2):
        if has_h0:
            h_s[d] = h0_ref[0, d].reshape(wx, SSD_STATE)
        else:
            h_s[d] = jnp.zeros((wx, SSD_STATE), F32)

    def conv_silu(ref, w_ref, bias_ref, c):
        off = pl.multiple_of(c * q, q)
        u = ref[0, pl.ds(off, q), :]
        prev = ref[0, pl.ds(jnp.maximum(off - 1, 0), 1), :] * (c > 0).astype(F32)
        nxt = ref[0, pl.ds(jnp.minimum(off + q, seq - 1), 1), :] * (c < nc - 1).astype(F32)
        up = jnp.where(row == 0, prev, pltpu.roll(u, 1, 0))
        un = jnp.where(row == q - 1, nxt, pltpu.roll(u, q - 1, 0))
        w = w_ref[...]
        return _silu(up * w[0:1] + u * w[1:2] + un * w[2:3] + bias_ref[...])

    def dt_terms(c):
        off = pl.multiple_of(c * q, q)
        dtr = dt_ref[0, pl.ds(off, q), :] + par[0:1]
        dtv = jnp.maximum(dtr, 0.0) + jnp.log1p(jnp.exp(-jnp.abs(dtr)))
        da = dtv * (-jnp.exp(par[1:2]))
        cum = sum(_dot(tri, part) for part in _split3(da))
        return dtv, da, cum

    def lanes4(v, base):
        out = v[:, base + hpg - 1:base + hpg]
        for hl in range(hpg - 2, -1, -1):
            out = jnp.where(lane_blk <= hl, v[:, base + hl:base + hl + 1], out)
        return out

    def rows4(v, base):
        out = v[:, base + hpg - 1:base + hpg]
        for hl in range(hpg - 2, -1, -1):
            out = jnp.where(row_blk <= hl, v[:, base + hl:base + hl + 1], out)
        return out

    def chunk(xc, b16, c16, dtv, acq, tot, base, mask, d):
        acr = acq.T
        s = lax.dot_general(c16, b16, NT_DIMS, preferred_element_type=F32)
        xdt = xc * lanes4(dtv, base)
        xdt16 = xdt.astype(BF16)
        yd = jnp.zeros((q, wx), F32)
        for hl in range(hpg):
            seg = acq[:, base + hl:base + hl + 1] - acr[base + hl:base + hl + 1, :]
            w = (s * jnp.exp(jnp.where(mask, seg, -jnp.inf))).astype(BF16)
            yd = yd + _dot(w, jnp.where(lane_blk == hl, xdt16, jnp.zeros_like(xdt16)))
        hs = h_s[d]
        yoff = lax.dot_general(c16, hs.astype(BF16), NT_DIMS, preferred_element_type=F32)
        yoff = yoff * lanes4(jnp.exp(acq), base)
        xdec = (xdt * lanes4(jnp.exp(tot - acq), base)).astype(BF16)
        contrib = lax.dot_general(xdec, b16, TN_DIMS, preferred_element_type=F32)
        h_s[d] = hs * rows4(jnp.exp(tot), base) + contrib
        return yd + yoff

    def fwd_body(c, carry):
        off = pl.multiple_of(c * q, q)
        xc = conv_silu(x_ref, cwx_ref, cbx_ref, c)
        b16 = conv_silu(b_ref, cwb_ref, cbb_ref, c).astype(BF16)
        c16 = conv_silu(c_ref, cwc_ref, cbc_ref, c).astype(BF16)
        xs_s[pl.ds(off, q), :] = xc
        bc_s[pl.ds(off, q), 0:SSD_STATE] = b16
        bc_s[pl.ds(off, q), SSD_STATE:2 * SSD_STATE] = c16
        dtv, _, cum = dt_terms(c)
        y = chunk(xc, b16, c16, dtv, cum, cum[q - 1:q, :], 0, lower, 0)
        y_ref[0, pl.ds(off, q), :] = y + xc * d_lanes
        return carry

    def bwd_body(i, carry):
        c = nc - 1 - i
        off = pl.multiple_of(c * q, q)
        xc = xs_s[pl.ds(off, q), :]
        b16 = bc_s[pl.ds(off, q), 0:SSD_STATE]
        c16 = bc_s[pl.ds(off, q), SSD_STATE:2 * SSD_STATE]
        dtv, da, cum = dt_terms(c)
        rev = cum[q - 1:q, :] - cum + da
        y = chunk(xc, b16, c16, dtv, rev, rev[0:1, :], hpg, upper, 1)
        y_ref[0, pl.ds(off, q), :] += y
        return carry

    lax.fori_loop(0, nc, fwd_body, 0)
    lax.fori_loop(0, nc, bwd_body, 0)
    if emit_state:
        for d in range(2):
            st_ref[0, d] = h_s[d].reshape(hpg, SSD_HEAD_DIM, SSD_STATE)


def _ssd_mixer(proj, conv_w, conv_b, par, h0, emit_state):
    b, l, _ = proj.shape
    g = SSD_GROUPS
    wx = HEADS_PER_GROUP * SSD_HEAD_DIM
    n = SSD_STATE
    nc = l // SSD_CHUNK
    has_h0 = h0 is not None

    def col(width, start):
        blk = start // width
        return pl.BlockSpec((1, l, width), lambda i, j: (i, 0, blk + j))

    def cw(width, start, rows):
        blk = start // width
        return pl.BlockSpec((rows, width), lambda i, j: (0, blk + j))

    in_specs = [
        col(wx, COL_X), col(n, COL_B), col(n, COL_C), col(LANES, COL_DT),
        cw(wx, 0, 3), cw(n, SSD_INNER, 3), cw(n, SSD_INNER + g * n, 3),
        cw(wx, 0, 1), cw(n, SSD_INNER, 1), cw(n, SSD_INNER + g * n, 1),
        pl.BlockSpec((1, 8, LANES), lambda i, j: (j, 0, 0)),
    ]
    args = [proj, proj, proj, proj, conv_w, conv_w, conv_w, conv_b, conv_b, conv_b, par]
    state_spec = pl.BlockSpec((1, 2, HEADS_PER_GROUP, SSD_HEAD_DIM, n), lambda i, j: (i, 0, j, 0, 0))
    if has_h0:
        in_specs.append(state_spec)
        args.append(h0)
    out_specs = [pl.BlockSpec((1, l, wx), lambda i, j: (i, 0, j))]
    out_shape = [jax.ShapeDtypeStruct((b, l, SSD_INNER), F32)]
    if emit_state:
        out_specs.append(state_spec)
        out_shape.append(jax.ShapeDtypeStruct((b, 2, SSD_HEADS, SSD_HEAD_DIM, n), F32))
    outs = pl.pallas_call(
        functools.partial(_ssd_kernel, nc=nc, seq=l, has_h0=has_h0, emit_state=emit_state),
        grid=(b, g),
        in_specs=in_specs,
        out_specs=out_specs,
        out_shape=out_shape,
        scratch_shapes=[
            pltpu.VMEM((l, wx), F32),
            pltpu.VMEM((l, 2 * n), BF16),
            pltpu.VMEM((2, wx, n), F32),
        ],
        compiler_params=_cparams(2),
        name="ssd_mixer",
    )(*args)
    return (outs[0], outs[1]) if emit_state else (outs[0], None)


def _rope(x, cos, sin_signed, pair):
    n = x.shape[1]
    lane = lax.broadcasted_iota(jnp.int32, (1, n), 1)
    first = (lane & (2 * pair - 1)) < pair
    partner = jnp.where(first, pltpu.roll(x, n - pair, 1), pltpu.roll(x, pair, 1))
    return x * cos + partner * sin_signed


def _tile_lanes(x, k):
    return jnp.concatenate([x] * k, axis=1)


def _qkv_kernel(*refs, nct, has_pos, emit_cache):
    it = iter(refs)
    mla_ref, gqa_ref = next(it), next(it)
    qg_ref, kvg_ref, wuq_ref, wk_ref, wv_ref, gq_ref, gk_ref = (next(it) for _ in range(7))
    if has_pos:
        mcos_ref, msin_ref, gcos_ref, gsin_ref = (next(it) for _ in range(4))
    if nct:
        cckv_ref, ckr_ref, cgk_ref, cgv_ref = (next(it) for _ in range(4))
    qm_ref, km_ref, vm_ref, qg_out, kg_ref, vg_ref = (next(it) for _ in range(6))
    if emit_cache:
        ckv_out, kpe_out, kn_out, v_out = (next(it) for _ in range(4))

    t = pl.program_id(1)
    lane = lax.broadcasted_iota(jnp.int32, (1, LANES), 1)
    first_half = lane < GQA_HEAD_DIM
    mla_scale = 1.0 / math.sqrt(MLA_NOPE + MLA_ROPE)
    gqa_scale = 1.0 / math.sqrt(GQA_HEAD_DIM)

    def write_mla_kv(ckv, kpe_r):
        c16 = ckv.astype(BF16)
        km_ref[0] = (_dot(c16, wk_ref[...]) + _tile_lanes(kpe_r, MLA_HEADS)).astype(BF16)
        vm_ref[0] = _dot(c16, wv_ref[...]).astype(BF16)

    def new_tile():
        m = mla_ref[0]
        q_lat = m[:, 0:MLA_Q_LORA]
        kv_lat = m[:, MLA_Q_LORA:MLA_Q_LORA + MLA_KV_LORA]
        kpe = m[:, MLA_Q_LORA + MLA_KV_LORA:]
        qn = _rms(q_lat) * qg_ref[...]
        qm = _dot(qn.astype(BF16), wuq_ref[...])
        ckv = _rms(kv_lat) * kvg_ref[...]
        kpe_r = kpe
        if has_pos:
            mcos, msin = mcos_ref[...], msin_ref[...]
            qm = _rope(qm, _tile_lanes(mcos, MLA_HEADS), _tile_lanes(msin, MLA_HEADS), MLA_ROPE // 4)
            kpe_r = _rope(kpe, mcos, msin, MLA_ROPE // 4)
        qm_ref[0] = (qm * mla_scale).astype(BF16)
        write_mla_kv(ckv, kpe_r)

        g = gqa_ref[0]
        if has_pos:
            gcos, gsin = gcos_ref[...], gsin_ref[...]
        for h in range(GQA_HEADS):
            xh = g[:, h * LANES:(h + 1) * LANES]
            qh = _rms(xh, GQA_HEAD_DIM) * gq_ref[...]
            if has_pos:
                qh = _rope(qh, gcos, gsin, GQA_HEAD_DIM // 4)
            qg_out[0, :, h * LANES:(h + 1) * LANES] = (qh * gqa_scale).astype(BF16)
        kns = []
        for j in range(GQA_KV_HEADS):
            c0 = GQA_HEADS * LANES + j * LANES
            kn = _rms(g[:, c0:c0 + LANES]) * gk_ref[...]
            kns.append(kn)
            kr = _rope(kn, gcos, gsin, GQA_HEAD_DIM // 4) if has_pos else kn
            kg_ref[0, :, j * LANES:(j + 1) * LANES] = kr.astype(BF16)
        v0 = (GQA_HEADS + GQA_KV_HEADS) * LANES
        v = g[:, v0:v0 + GQA_KV_HEADS * LANES]
        vg_ref[0] = v.astype(BF16)
        if emit_cache:
            ckv_out[0] = ckv
            kpe_out[0] = kpe[:, MLA_NOPE:MLA_NOPE + MLA_ROPE]
            kn_out[0] = jnp.where(first_half, kns[0], kns[1])
            v_out[0] = jnp.where(first_half, v[:, 0:LANES], v[:, LANES:2 * LANES])

    def ctx_tile():
        write_mla_kv(cckv_ref[0], ckr_ref[0])
        for src, dst in ((cgk_ref, kg_ref), (cgv_ref, vg_ref)):
            x = src[0]
            sw = pltpu.roll(x, GQA_HEAD_DIM, 1)
            dst[0, :, 0:LANES] = jnp.where(first_half, x, sw).astype(BF16)
            dst[0, :, LANES:2 * LANES] = jnp.where(first_half, sw, x).astype(BF16)

    if nct:
        pl.when(t < nct)(ctx_tile)
        pl.when(t >= nct)(new_tile)
    else:
        new_tile()


def _attention_operands(proj, wts, tables, ctx, layer, emit_cache, tm):
    b, l, _ = proj.shape
    has_pos = tables is not None
    nct = 0 if ctx is None else ctx[0].shape[2] // tm
    lk = l + nct * tm
    nt = lk // tm

    def new_t(t):
        return jnp.maximum(t - nct, 0) if nct else t

    in_specs = [
        pl.BlockSpec((1, tm, 512), lambda i, t: (i, new_t(t), COL_MLA // 512)),
        pl.BlockSpec((1, tm, 1024), lambda i, t: (i, new_t(t), COL_GQA // 1024)),
    ] + [_const_spec(w.shape) for w in wts]
    args = [proj, proj] + list(wts)
    if has_pos:
        in_specs += [pl.BlockSpec((tm, LANES), lambda i, t: (new_t(t), 0))] * 4
        args += list(tables)
    if nct:
        in_specs += [pl.BlockSpec((1, None, tm, LANES), lambda i, t: (i, layer, jnp.minimum(t, nct - 1), 0))] * 4
        args += list(ctx)
    q_spec = lambda w: pl.BlockSpec((1, tm, w), lambda i, t: (i, new_t(t), 0))
    k_spec = lambda w: pl.BlockSpec((1, tm, w), lambda i, t: (i, t, 0))
    out_specs = [q_spec(512), k_spec(512), k_spec(256), q_spec(512), k_spec(256), k_spec(256)]
    out_shape = [
        jax.ShapeDtypeStruct((b, l, 512), BF16), jax.ShapeDtypeStruct((b, lk, 512), BF16),
        jax.ShapeDtypeStruct((b, lk, 256), BF16), jax.ShapeDtypeStruct((b, l, 512), BF16),
        jax.ShapeDtypeStruct((b, lk, 256), BF16), jax.ShapeDtypeStruct((b, lk, 256), BF16),
    ]
    if emit_cache:
        out_specs += [q_spec(MLA_KV_LORA), q_spec(MLA_ROPE), q_spec(LANES), q_spec(LANES)]
        out_shape += [
            jax.ShapeDtypeStruct((b, l, MLA_KV_LORA), F32), jax.ShapeDtypeStruct((b, l, MLA_ROPE), F32),
            jax.ShapeDtypeStruct((b, l, LANES), F32), jax.ShapeDtypeStruct((b, l, LANES), F32),
        ]
    return pl.pallas_call(
        functools.partial(_qkv_kernel, nct=nct, has_pos=has_pos, emit_cache=emit_cache),
        grid=(b, nt),
        in_specs=in_specs,
        out_specs=out_specs,
        out_shape=out_shape,
        compiler_params=_cparams(2),
        name="attention_operands",
    )(*args)


def _attn_kernel(q_ref, k_ref, v_ref, o_ref, *, kc, nkc, shared_k):
    tq = q_ref.shape[1]
    lane = lax.broadcasted_iota(jnp.int32, (1, LANES), 1)
    outs = []
    for a in range(2):
        qa = q_ref[0, :, a * LANES:(a + 1) * LANES]

        def body(i, carry, a=a, qa=qa):
            m, l, acc = carry
            off = pl.multiple_of(i * kc, kc)
            if shared_k:
                k = k_ref[0, pl.ds(off, kc), :]
            else:
                k = k_ref[0, pl.ds(off, kc), a * LANES:(a + 1) * LANES]
            v = v_ref[0, pl.ds(off, kc), :]
            s = lax.dot_general(qa, k, NT_DIMS, preferred_element_type=F32)
            m_new = jnp.maximum(m, jnp.max(s, axis=-1, keepdims=True))
            alpha = jnp.exp(m - m_new)
            p = jnp.exp(s - m_new)
            l = alpha * l + jnp.sum(p, axis=-1, keepdims=True)
            acc = alpha * acc + _dot(p.astype(BF16), v)
            return m_new, l, acc

        init = (jnp.full((tq, 1), -jnp.inf, F32), jnp.zeros((tq, 1), F32), jnp.zeros((tq, LANES), F32))
        _, l, acc = lax.fori_loop(0, nkc, body, init)
        outs.append(acc / l)
    o_ref[0] = jnp.where(lane < GQA_HEAD_DIM, outs[0], outs[1]).astype(BF16)


def _attention(q, k, v, shared_k, tq, kc):
    b, l, _ = q.shape
    lk = k.shape[1]
    kw = LANES if shared_k else 2 * LANES
    return pl.pallas_call(
        functools.partial(_attn_kernel, kc=kc, nkc=lk // kc, shared_k=shared_k),
        grid=(b, 2, l // tq),
        in_specs=[
            pl.BlockSpec((1, tq, 2 * LANES), lambda i, j, t: (i, t, j)),
            pl.BlockSpec((1, lk, kw), lambda i, j, t: (i, 0, j)),
            pl.BlockSpec((1, lk, LANES), lambda i, j, t: (i, 0, j)),
        ],
        out_specs=pl.BlockSpec((1, tq, LANES), lambda i, j, t: (i, t, j)),
        out_shape=jax.ShapeDtypeStruct((b, l, 2 * LANES), BF16),
        compiler_params=_cparams(3),
        name="attention_shared_k" if shared_k else "attention",
    )(q, k, v)


def _post_kernel(*refs, moe):
    it = iter(refs)
    ys_ref, z_ref, ym_ref, yg_ref, x_ref, g1_ref, sh2_ref, sc2_ref = (next(it) for _ in range(8))
    ng_ref, gpost_ref, gpre_ref, wout_ref = (next(it) for _ in range(4))
    wr_ref = next(it) if moe else None
    x1_ref, h2_ref = next(it), next(it)
    comb_ref = next(it) if moe else None

    y = _rms(ys_ref[0] * _silu(z_ref[0])) * ng_ref[...]
    cat = jnp.concatenate([y.astype(BF16), ym_ref[0], yg_ref[0]], axis=1)
    mix = _dot(cat, wout_ref[...])
    x1 = x_ref[0] + g1_ref[0] * (_rms(mix) * gpost_ref[...])
    x1_ref[0] = x1
    h2 = _rms(x1) * gpre_ref[...] * (1.0 + sc2_ref[0]) + sh2_ref[0]
    h16 = h2.astype(BF16)
    h2_ref[0] = h16
    if moe:
        w_hi = wr_ref[...].astype(BF16)
        w_lo = (wr_ref[...] - w_hi.astype(F32)).astype(BF16)
        logits = _dot(h16, w_hi) + _dot(h16, w_lo)
        lane = lax.broadcasted_iota(jnp.int32, (1, LANES), 1).astype(F32)
        lg = jnp.where(lane < N_EXPERTS, logits, -jnp.inf)
        m1 = jnp.max(lg, axis=-1, keepdims=True)
        i1 = jnp.min(jnp.where(lg == m1, lane, float(LANES)), axis=-1, keepdims=True)
        lg2 = jnp.where(lane == i1, -jnp.inf, lg)
        m2 = jnp.max(lg2, axis=-1, keepdims=True)
        i2 = jnp.min(jnp.where(lg2 == m2, lane, float(LANES)), axis=-1, keepdims=True)
        e = jnp.exp(m2 - m1)
        comb_ref[0] = jnp.where(lane == i1, 1.0 / (1.0 + e), jnp.where(lane == i2, e / (1.0 + e), 0.0))


def _post_attention(yssd, proj, ym, yg, x, mod, mod_row, ng, gpost, gpre, w_out, w_router, tm):
    b, l, d = x.shape
    moe = w_router is not None
    tok = lambda w: pl.BlockSpec((1, tm, w), lambda i, t: (i, t, 0))
    modk = lambda k: pl.BlockSpec((1, 1, d), lambda i, t: (mod_row(i), 0, k))
    in_specs = [tok(SSD_INNER), tok(SSD_INNER), tok(256), tok(256), tok(d), modk(2), modk(3), modk(4),
                _const_spec(ng.shape), _const_spec(gpost.shape), _const_spec(gpre.shape),
                _const_spec(w_out.shape)]
    args = [yssd, proj, ym, yg, x, mod, mod, mod, ng, gpost, gpre, w_out]
    out_specs = [tok(d), tok(d)]
    out_shape = [jax.ShapeDtypeStruct((b, l, d), F32), jax.ShapeDtypeStruct((b, l, d), BF16)]
    if moe:
        in_specs.append(_const_spec(w_router.shape))
        args.append(w_router)
        out_specs.append(tok(LANES))
        out_shape.append(jax.ShapeDtypeStruct((b, l, LANES), F32))
    outs = pl.pallas_call(
        functools.partial(_post_kernel, moe=moe),
        grid=(b, l // tm),
        in_specs=in_specs,
        out_specs=out_specs,
        out_shape=out_shape,
        compiler_params=_cparams(2),
        name="post_attention",
    )(*args)
    return outs if moe else (outs[0], outs[1], None)


def _ffn_kernel(h_ref, x1_ref, g2_ref, gp_ref, wg_ref, wu_ref, wd_ref, o_ref):
    h = h_ref[0]
    act = (_silu(_dot(h, wg_ref[...])) * _dot(h, wu_ref[...])).astype(BF16)
    f = _dot(act, wd_ref[...])
    o_ref[0] = x1_ref[0] + g2_ref[0] * (_rms(f) * gp_ref[...])


def _dense_ffn(h2, x1, mod, mod_row, gp, wg, wu, wd, tm):
    b, l, d = x1.shape
    tok = pl.BlockSpec((1, tm, d), lambda i, t: (i, t, 0))
    return pl.pallas_call(
        _ffn_kernel,
        grid=(b, l // tm),
        in_specs=[tok, tok, pl.BlockSpec((1, 1, d), lambda i, t: (mod_row(i), 0, 5)),
                  _const_spec(gp.shape), _const_spec(wg.shape), _const_spec(wu.shape), _const_spec(wd.shape)],
        out_specs=tok,
        out_shape=jax.ShapeDtypeStruct((b, l, d), F32),
        compiler_params=_cparams(2),
        name="dense_ffn",
    )(h2, x1, mod, gp, wg, wu, wd)


def _moe_kernel(h_ref, comb_ref, x1_ref, g2_ref, gp_ref, wg_ref, wu_ref, wd_ref, o_ref, acc_s):
    e = pl.program_id(2)

    @pl.when(e == 0)
    def _():
        acc_s[...] = jnp.zeros_like(acc_s)

    h = h_ref[0]
    act = (_silu(_dot(h, wg_ref[0])) * _dot(h, wu_ref[0])).astype(BF16)
    y = _dot(act, wd_ref[0])
    lane = lax.broadcasted_iota(jnp.int32, (1, LANES), 1)
    ce = jnp.sum(jnp.where(lane == e, comb_ref[0], 0.0), axis=-1, keepdims=True)
    acc_s[...] += ce * y

    @pl.when(e == pl.num_programs(2) - 1)
    def _():
        o_ref[0] = x1_ref[0] + g2_ref[0] * (_rms(acc_s[...]) * gp_ref[...])


def _moe_ffn(h2, comb, x1, mod, mod_row, gp, wg, wu, wd, tm):
    b, l, d = x1.shape
    n_e, _, ff = wg.shape
    tok = lambda w: pl.BlockSpec((1, tm, w), lambda i, t, e: (i, t, 0))
    return pl.pallas_call(
        _moe_kernel,
        grid=(b, l // tm, n_e),
        in_specs=[tok(d), tok(LANES), tok(d),
                  pl.BlockSpec((1, 1, d), lambda i, t, e: (mod_row(i), 0, 5)),
                  pl.BlockSpec((1, d), lambda i, t, e: (0, 0)),
                  pl.BlockSpec((1, d, ff), lambda i, t, e: (e, 0, 0)),
                  pl.BlockSpec((1, d, ff), lambda i, t, e: (e, 0, 0)),
                  pl.BlockSpec((1, ff, d), lambda i, t, e: (e, 0, 0))],
        out_specs=tok(d),
        out_shape=jax.ShapeDtypeStruct((b, l, d), F32),
        scratch_shapes=[pltpu.VMEM((tm, d), F32)],
        compiler_params=_cparams(3),
        name="moe_ffn",
    )(h2, comb, x1, mod, gp, wg, wu, wd)


def _in_proj_columns():
    o_dt = 1536
    o_ql, o_kv, o_kpe = 1552, 1808, 1936
    o_gq, o_gk, o_gv = 1968, 2224, 2352
    idx = np.full((IN_COLS_PADDED,), -1, np.int64)
    idx[0:1536] = np.arange(1536)
    idx[COL_MLA:COL_MLA + 256] = o_ql + np.arange(256)
    idx[COL_MLA + 256:COL_MLA + 384] = o_kv + np.arange(128)
    idx[COL_MLA + 384 + MLA_NOPE:COL_MLA + 384 + MLA_NOPE + MLA_ROPE] = o_kpe + np.arange(MLA_ROPE)
    for h in range(GQA_HEADS):
        idx[COL_GQA + h * LANES:COL_GQA + h * LANES + 64] = o_gq + h * 64 + np.arange(64)
    for j in range(GQA_KV_HEADS):
        for r in range(2):
            c0 = COL_GQA + 512 + j * LANES + r * 64
            idx[c0:c0 + 64] = o_gk + j * 64 + np.arange(64)
            idx[c0 + 256:c0 + 256 + 64] = o_gv + j * 64 + np.arange(64)
    for g in range(SSD_GROUPS):
        for d in range(2):
            c0 = COL_DT + g * LANES + d * HEADS_PER_GROUP
            idx[c0:c0 + HEADS_PER_GROUP] = o_dt + d * SSD_HEADS + g * HEADS_PER_GROUP + np.arange(HEADS_PER_GROUP)
    return idx


def _gather_cols(w, idx):
    cols = jnp.take(w, jnp.asarray(np.maximum(idx, 0)), axis=1)
    return jnp.where(jnp.asarray(idx >= 0)[None, :], cols, 0.0)


def _mla_weight_columns():
    per_q = MLA_NOPE + MLA_ROPE
    uq = np.full((MLA_HEADS * LANES,), -1, np.int64)
    uk = np.full((MLA_HEADS * LANES,), -1, np.int64)
    uv = np.zeros((MLA_HEADS * MLA_V,), np.int64)
    for h in range(MLA_HEADS):
        uq[h * LANES:h * LANES + per_q] = h * per_q + np.arange(per_q)
        uk[h * LANES:h * LANES + MLA_NOPE] = h * (MLA_NOPE + MLA_V) + np.arange(MLA_NOPE)
        uv[h * MLA_V:(h + 1) * MLA_V] = h * (MLA_NOPE + MLA_V) + MLA_NOPE + np.arange(MLA_V)
    return uq, uk, uv


def _rope_tables(n_tokens):
    t = np.arange(n_tokens)
    row, colp = (t // GRID_W).astype(np.float64), (t % GRID_W).astype(np.float64)

    def table(width, lane0, head_dim, reps):
        cos = np.ones((n_tokens, width))
        sin = np.zeros((n_tokens, width))
        half = head_dim // 2
        quarter = half // 2
        for dim in range(head_dim):
            pos = row if dim < half else colp
            inv = ROPE_BASE ** (-(2.0 * (dim % quarter)) / half)
            ang = pos * inv
            sign = -1.0 if (dim % half) < quarter else 1.0
            for r in range(reps):
                cos[:, lane0 + r * head_dim + dim] = np.cos(ang)
                sin[:, lane0 + r * head_dim + dim] = sign * np.sin(ang)
        return jnp.asarray(cos, F32), jnp.asarray(sin, F32)

    mcos, msin = table(LANES, MLA_NOPE, MLA_ROPE, 1)
    gcos, gsin = table(LANES, 0, GQA_HEAD_DIM, 2)
    return mcos, msin, gcos, gsin


def _trunk_layer(i, x, mod_i, mod_row, p, tables, ctx):
    b, l, d = x.shape
    tm = min(512, l)
    emit = ctx is None
    proj = _in_projection(x, mod_i, mod_row, p["g_mix_pre"][i], p["w_in"][i], tm)
    h0 = None if ctx is None else ctx["state"][:, i]
    yssd, state = _ssd_mixer(proj, p["conv_w"][i], p["conv_b"][i], p["ssd_par"][i], h0, emit)
    ctx_kv = None if ctx is None else ctx["kv"]
    ops = _attention_operands(proj, p["attn_w"][i], tables, ctx_kv, i, emit, tm)
    qm, km, vm, qg, kg, vg = ops[:6]
    lk = km.shape[1]
    tq = min(256, l)
    kc = min(512, lk)
    ym = _attention(qm, km, vm, False, tq, kc)
    yg = _attention(qg, kg, vg, True, tq, kc)
    moe = i % 2 == 1
    j = i // 2
    x1, h2, comb = _post_attention(
        yssd, proj, ym, yg, x, mod_i, mod_row, p["ssd_norm_g"][i], p["g_mix_post"][i], p["g_ffn_pre"][i],
        p["w_out"][i], p["w_router"][j] if moe else None, tm)
    if moe:
        x2 = _moe_ffn(h2, comb, x1, mod_i, mod_row, p["g_ffn_post"][i],
                      p["moe_wg"][j], p["moe_wu"][j], p["moe_wd"][j], tm)
    else:
        x2 = _dense_ffn(h2, x1, mod_i, mod_row, p["g_ffn_post"][i],
                        p["ffn_wg"][j], p["ffn_wu"][j], p["ffn_wd"][j], tm)
    return x2, (state,) + tuple(ops[6:])


def kernel(x_prompt, x_sample, state_ssd, cache_mla_ckv, cache_mla_krope, cache_gqa_k, cache_gqa_v, c, c_ctx, w_mod, b_mod, g_mix_pre, g_mix_post, g_ffn_pre, g_ffn_post, w_in, ssd_conv_w, ssd_conv_b, ssd_A_log, ssd_dt_bias, ssd_D, ssd_norm_g, mla_q_norm_g, mla_w_uq, mla_kv_norm_g, mla_w_ukv, gqa_q_norm_g, gqa_k_norm_g, w_out, ffn_w_gate, ffn_w_up, ffn_w_down, moe_w_router, moe_w_gate, moe_w_up, moe_w_down):
    depth = w_in.shape[0]
    d = x_prompt.shape[-1]
    n_dec = x_sample.shape[0]
    hpg = HEADS_PER_GROUP

    in_idx = _in_proj_columns()
    uq_idx, uk_idx, uv_idx = _mla_weight_columns()
    row2 = lambda a: a.reshape(depth, 1, a.shape[-1])
    par = jnp.zeros((depth, SSD_GROUPS, 8, LANES), F32)
    for g in range(SSD_GROUPS):
        hs = slice(g * hpg, (g + 1) * hpg)
        par = par.at[:, g, 0, 0:hpg].set(ssd_dt_bias[:, 0, hs]).at[:, g, 0, hpg:2 * hpg].set(ssd_dt_bias[:, 1, hs])
        par = par.at[:, g, 1, 0:hpg].set(ssd_A_log[:, 0, hs]).at[:, g, 1, hpg:2 * hpg].set(ssd_A_log[:, 1, hs])
        par = par.at[:, g, 2, 0:hpg].set(ssd_D[:, hs])
    zeros64 = jnp.zeros((depth, 1, GQA_HEAD_DIM), F32)
    attn_w = []
    for i in range(depth):
        attn_w.append((
            mla_q_norm_g[i][None, :], mla_kv_norm_g[i][None, :],
            _gather_cols(mla_w_uq[i], uq_idx).astype(BF16),
            _gather_cols(mla_w_ukv[i], uk_idx).astype(BF16),
            jnp.take(mla_w_ukv[i], jnp.asarray(uv_idx), axis=1).astype(BF16),
            jnp.concatenate([gqa_q_norm_g[i][None, :], zeros64[i]], axis=1),
            jnp.concatenate([gqa_k_norm_g[i][None, :]] * 2, axis=1),
        ))
    p = dict(
        g_mix_pre=row2(g_mix_pre), g_mix_post=row2(g_mix_post), g_ffn_pre=row2(g_ffn_pre),
        g_ffn_post=row2(g_ffn_post), ssd_norm_g=row2(ssd_norm_g),
        w_in=[_gather_cols(w_in[i], in_idx).astype(BF16) for i in range(depth)],
        conv_w=ssd_conv_w, conv_b=row2(ssd_conv_b), ssd_par=par, attn_w=attn_w,
        w_out=w_out.astype(BF16),
        ffn_wg=ffn_w_gate.astype(BF16), ffn_wu=ffn_w_up.astype(BF16), ffn_wd=ffn_w_down.astype(BF16),
        w_router=jnp.pad(moe_w_router, ((0, 0), (0, 0), (0, LANES - N_EXPERTS))),
        moe_wg=moe_w_gate.astype(BF16), moe_wu=moe_w_up.astype(BF16), moe_wd=moe_w_down.astype(BF16),
    )

    rows = -(-(1 + n_dec) // 8) * 8
    c_all = jnp.concatenate([c_ctx[None, :], c, jnp.zeros((rows - 1 - n_dec, d), F32)], axis=0)
    mod = _modulation(c_all, w_mod, b_mod).reshape(depth, rows, 1, 6 * d)

    y = x_prompt
    collected = [[], [], [], [], []]
    for i in range(depth):
        y, outs = _trunk_layer(i, y, mod[i], lambda bi: 0, p, None, None)
        for lst, t in zip(collected, outs):
            lst.append(t)
    y_prompt = y
    bsz, seq = x_prompt.shape[:2]
    new_state = jnp.stack(collected[0], axis=1)
    new_ckv = jnp.stack(collected[1], axis=1)
    new_krope = jnp.stack(collected[2], axis=1)
    new_k = jnp.stack(collected[3], axis=1).reshape(bsz, depth, seq, GQA_KV_HEADS, GQA_HEAD_DIM)
    new_v = jnp.stack(collected[4], axis=1).reshape(bsz, depth, seq, GQA_KV_HEADS, GQA_HEAD_DIM)

    past = cache_mla_ckv.shape[2]
    krope_blk = jnp.pad(cache_mla_krope, ((0, 0), (0, 0), (0, 0), (MLA_NOPE, LANES - MLA_NOPE - MLA_ROPE)))
    ctx = dict(
        state=state_ssd,
        kv=(cache_mla_ckv, krope_blk,
            cache_gqa_k.reshape(n_dec, depth, past, LANES), cache_gqa_v.reshape(n_dec, depth, past, LANES)),
    )
    tables = _rope_tables(x_sample.shape[1])
    y = x_sample
    for i in range(depth):
        y, _ = _trunk_layer(i, y, mod[i], lambda bi: bi + 1, p, tables, ctx)
    return (y_prompt, y, new_state, new_ckv, new_krope, new_k, new_v)
```

```python
import functools
import itertools
import math

import numpy as np
import jax
import jax.numpy as jnp
from jax import lax
from jax.experimental import pallas as pl
from jax.experimental.pallas import tpu as pltpu

F32 = jnp.float32
BF16 = jnp.bfloat16

EPS = 1e-6
ROPE_BASE = 10000.0
GRID_W = 64
SSD_HEAD_DIM = 64
SSD_HEADS = 8
SSD_GROUPS = 2
SSD_STATE = 128
SSD_CHUNK = 128
SSD_INNER = SSD_HEADS * SSD_HEAD_DIM
HEADS_PER_GROUP = SSD_HEADS // SSD_GROUPS
SSD_ROWS = 16
MLA_HEADS = 4
MLA_V = 64
MLA_NOPE = 64
MLA_ROPE = 32
MLA_Q_LORA = 256
MLA_KV_LORA = 128
GQA_HEADS = 4
GQA_KV_HEADS = 2
GQA_HEAD_DIM = 64
N_EXPERTS = 8

LANES = 128

COL_Z = 0
COL_X = 512
COL_B = 1024
COL_C = 1280
COL_MLA = 1536
COL_GQA = 2048
COL_DT = 3072
IN_COLS_PADDED = 3328

VMEM_LIMIT = 56 * 1024 * 1024

NT_DIMS = (((1,), (1,)), ((), ()))
TN_DIMS = (((0,), (0,)), ((), ()))


def _cparams(n_grid):
    return pltpu.CompilerParams(
        dimension_semantics=("arbitrary",) * n_grid, vmem_limit_bytes=VMEM_LIMIT)


def _const_spec(shape):
    nd = len(shape)
    return pl.BlockSpec(shape, lambda *_: (0,) * nd, pipeline_mode=pl.Buffered(1))


def _dot(a, b):
    return jnp.dot(a, b, preferred_element_type=F32)


def _rms(x, width=None):
    n = x.shape[-1] if width is None else width
    return x * lax.rsqrt(jnp.sum(x * x, axis=-1, keepdims=True) * (1.0 / n) + EPS)


def _silu(x):
    return x * jax.nn.sigmoid(x)


def _split3(v):
    hi = v.astype(BF16)
    r = v - hi.astype(F32)
    mid = r.astype(BF16)
    lo = (r - mid.astype(F32)).astype(BF16)
    return hi, mid, lo


def _mod_kernel(c_ref, w_ref, b_ref, o_ref):
    s = _silu(c_ref[...]).astype(BF16)
    o_ref[0] = _dot(s, w_ref[0].astype(BF16)) + b_ref[0]


def _modulation(c_all, w_mod, b_mod):
    depth, d, n = w_mod.shape
    tn = 1536
    rows = c_all.shape[0]
    return pl.pallas_call(
        _mod_kernel,
        grid=(depth, n // tn),
        in_specs=[
            pl.BlockSpec((rows, d), lambda i, j: (0, 0)),
            pl.BlockSpec((1, d, tn), lambda i, j: (i, 0, j)),
            pl.BlockSpec((1, 1, tn), lambda i, j: (i, 0, j)),
        ],
        out_specs=pl.BlockSpec((1, rows, tn), lambda i, j: (i, 0, j)),
        out_shape=jax.ShapeDtypeStruct((depth, rows, n), F32),
        compiler_params=_cparams(2),
        name="modulation",
    )(c_all, w_mod, b_mod.reshape(depth, 1, n))


def _inproj_kernel(x_ref, sh_ref, sc_ref, g_ref, w_ref, o_ref):
    h = _rms(x_ref[0]) * g_ref[...]
    h = h * (1.0 + sc_ref[0]) + sh_ref[0]
    o_ref[0] = _dot(h.astype(BF16), w_ref[...])


def _in_projection(x, mod, mod_row, gain, w_in_p, tm):
    b, l, d = x.shape
    n = w_in_p.shape[1]
    return pl.pallas_call(
        _inproj_kernel,
        grid=(b, l // tm),
        in_specs=[
            pl.BlockSpec((1, tm, d), lambda i, t: (i, t, 0)),
            pl.BlockSpec((1, 1, d), lambda i, t: (mod_row(i), 0, 0)),
            pl.BlockSpec((1, 1, d), lambda i, t: (mod_row(i), 0, 1)),
            _const_spec((1, d)),
            _const_spec((d, n)),
        ],
        out_specs=pl.BlockSpec((1, tm, n), lambda i, t: (i, t, 0)),
        out_shape=jax.ShapeDtypeStruct((b, l, n), F32),
        compiler_params=_cparams(2),
        name="in_projection",
    )(x, mod, mod, gain, w_in_p)


def _ssd_kernel(*refs, nc, seq, has_h0, emit_state):
    it = iter(refs)
    x_ref, b_ref, c_ref, dt_ref = next(it), next(it), next(it), next(it)
    cwx_ref, cwb_ref, cwc_ref = next(it), next(it), next(it)
    cbx_ref, cbb_ref, cbc_ref = next(it), next(it), next(it)
    par_ref = next(it)
    h0_ref = next(it) if has_h0 else None
    y_ref = next(it)
    st_ref = next(it) if emit_state else None
    xt_s, yd_s, bc_s, dtv_s, arow_s, acol_s, h_s = (next(it) for _ in range(7))

    q = SSD_CHUNK
    hpg = HEADS_PER_GROUP
    hd = SSD_HEAD_DIM
    wx = hpg * hd
    r = SSD_ROWS
    row = lax.broadcasted_iota(jnp.int32, (q, 1), 0)
    si = lax.broadcasted_iota(jnp.int32, (q, q), 0)
    ti = lax.broadcasted_iota(jnp.int32, (q, q), 1)
    le = si <= ti
    ge = si >= ti
    tri_le = jnp.where(le, 1.0, 0.0).astype(BF16)
    tri_ge = jnp.where(ge, 1.0, 0.0).astype(BF16)
    par = par_ref[0]
    bias_t, alog_t, d_t = par[0:r], par[r:2 * r], par[2 * r:3 * r]

    for d in range(2):
        if has_h0:
            h_s[d] = h0_ref[0, d].reshape(wx, SSD_STATE)
        else:
            h_s[d] = jnp.zeros((wx, SSD_STATE), F32)

    def conv_silu(ref, w_ref, bias_ref, c):
        off = pl.multiple_of(c * q, q)
        u = ref[0, pl.ds(off, q), :]
        prev = ref[0, pl.ds(jnp.maximum(off - 1, 0), 1), :] * (c > 0).astype(F32)
        nxt = ref[0, pl.ds(jnp.minimum(off + q, seq - 1), 1), :] * (c < nc - 1).astype(F32)
        up = jnp.where(row == 0, prev, pltpu.roll(u, 1, 0))
        un = jnp.where(row == q - 1, nxt, pltpu.roll(u, q - 1, 0))
        w = w_ref[...]
        return _silu(up * w[0:1] + u * w[1:2] + un * w[2:3] + bias_ref[...])

    tri_rows = jnp.concatenate([tri_le, tri_ge], axis=1)
    tri_cols = jnp.concatenate([tri_ge, tri_le], axis=0)
    fwd_row = lax.broadcasted_iota(jnp.int32, (r, 1), 0) < hpg
    fwd_lane = lax.broadcasted_iota(jnp.int32, (1, r), 1) < hpg

    def prep_body(c, carry):
        off = pl.multiple_of(c * q, q)
        xt = conv_silu(x_ref, cwx_ref, cbx_ref, c).T
        xt_s[c] = xt
        bc_s[pl.ds(off, q), 0:SSD_STATE] = conv_silu(b_ref, cwb_ref, cbb_ref, c).astype(BF16)
        bc_s[pl.ds(off, q), SSD_STATE:2 * SSD_STATE] = conv_silu(c_ref, cwc_ref, cbc_ref, c).astype(BF16)
        dtr = dt_ref[0, pl.ds(off, q), :].T[0:r, :] + bias_t
        dtv = jnp.maximum(dtr, 0.0) + jnp.log1p(jnp.exp(-jnp.abs(dtr)))
        dtv_s[c] = dtv
        parts = _split3(dtv * (-jnp.exp(alog_t)))
        rows = sum(_dot(part, tri_rows) for part in parts)
        cols = sum(lax.dot_general(tri_cols, part, NT_DIMS, preferred_element_type=F32) for part in parts)
        arow_s[c] = jnp.where(fwd_row, rows[:, 0:q], rows[:, q:2 * q])
        acol_s[c] = jnp.where(fwd_lane, cols[0:q, :], cols[q:2 * q, :])
        return carry

    def chunk(c, d):
        off = pl.multiple_of(c * q, q)
        base, mask = (0, le) if d == 0 else (hpg, ge)
        xt = xt_s[c]
        b16 = bc_s[pl.ds(off, q), 0:SSD_STATE]
        c16 = bc_s[pl.ds(off, q), SSD_STATE:2 * SSD_STATE]
        dtv, a_rows, a_cols = dtv_s[c], arow_s[c], acol_s[c]
        tot = a_rows[:, q - 1:q] if d == 0 else a_rows[:, 0:1]
        st = lax.dot_general(b16, c16, NT_DIMS, preferred_element_type=F32)
        hs = h_s[d]
        yoff = lax.dot_general(hs.astype(BF16), c16, NT_DIMS, preferred_element_type=F32)
        yield
        outs, xdecs, hnew = [], [], []
        for hl in range(hpg):
            k = base + hl
            rs = slice(hl * hd, (hl + 1) * hd)
            a_t = a_rows[k:k + 1, :]
            tot_k = tot[k:k + 1, :]
            seg = a_t - a_cols[:, k:k + 1]
            w = (st * jnp.exp(jnp.where(mask, seg, -jnp.inf))).astype(BF16)
            xdt = xt[rs, :] * dtv[k:k + 1, :]
            outs.append(_dot(xdt.astype(BF16), w) + yoff[rs, :] * jnp.exp(a_t))
            xdecs.append((xdt * jnp.exp(tot_k - a_t)).astype(BF16))
            hnew.append(hs[rs, :] * jnp.exp(tot_k))
            yield
        h_s[d] = jnp.concatenate(hnew, axis=0) + _dot(jnp.concatenate(xdecs, axis=0), b16)
        yd_s[d, c] = jnp.concatenate(outs, axis=0)

    def scan_body(i, carry):
        for _ in itertools.zip_longest(chunk(i, 0), chunk(nc - 1 - i, 1)):
            pass
        return carry

    def out_body(c, carry):
        xt = xt_s[c]
        skip = [xt[hl * hd:(hl + 1) * hd, :] * d_t[hl:hl + 1, :] for hl in range(hpg)]
        yt = yd_s[0, c] + yd_s[1, c] + jnp.concatenate(skip, axis=0)
        y_ref[0, pl.ds(pl.multiple_of(c * q, q), q), :] = yt.T
        return carry

    lax.fori_loop(0, nc, prep_body, 0)
    lax.fori_loop(0, nc, scan_body, 0)
    lax.fori_loop(0, nc, out_body, 0, unroll=4)
    if emit_state:
        for d in range(2):
            st_ref[0, d] = h_s[d].reshape(hpg, SSD_HEAD_DIM, SSD_STATE)


def _ssd_mixer(proj, conv_w, conv_b, par, h0, emit_state):
    b, l, _ = proj.shape
    g = SSD_GROUPS
    wx = HEADS_PER_GROUP * SSD_HEAD_DIM
    n = SSD_STATE
    nc = l // SSD_CHUNK
    has_h0 = h0 is not None

    def col(width, start):
        blk = start // width
        return pl.BlockSpec((1, l, width), lambda i, j: (i, 0, blk + j))

    def cw(width, start, rows):
        blk = start // width
        return pl.BlockSpec((rows, width), lambda i, j: (0, blk + j))

    in_specs = [
        col(wx, COL_X), col(n, COL_B), col(n, COL_C), col(LANES, COL_DT),
        cw(wx, 0, 3), cw(n, SSD_INNER, 3), cw(n, SSD_INNER + g * n, 3),
        cw(wx, 0, 1), cw(n, SSD_INNER, 1), cw(n, SSD_INNER + g * n, 1),
        pl.BlockSpec((1, 3 * SSD_ROWS, LANES), lambda i, j: (j, 0, 0)),
    ]
    args = [proj, proj, proj, proj, conv_w, conv_w, conv_w, conv_b, conv_b, conv_b, par]
    state_spec = pl.BlockSpec((1, 2, HEADS_PER_GROUP, SSD_HEAD_DIM, n), lambda i, j: (i, 0, j, 0, 0))
    if has_h0:
        in_specs.append(state_spec)
        args.append(h0)
    out_specs = [pl.BlockSpec((1, l, wx), lambda i, j: (i, 0, j))]
    out_shape = [jax.ShapeDtypeStruct((b, l, SSD_INNER), F32)]
    if emit_state:
        out_specs.append(state_spec)
        out_shape.append(jax.ShapeDtypeStruct((b, 2, SSD_HEADS, SSD_HEAD_DIM, n), F32))
    outs = pl.pallas_call(
        functools.partial(_ssd_kernel, nc=nc, seq=l, has_h0=has_h0, emit_state=emit_state),
        grid=(b, g),
        in_specs=in_specs,
        out_specs=out_specs,
        out_shape=out_shape,
        scratch_shapes=[
            pltpu.VMEM((nc, wx, SSD_CHUNK), F32),
            pltpu.VMEM((2, nc, wx, SSD_CHUNK), F32),
            pltpu.VMEM((l, 2 * n), BF16),
            pltpu.VMEM((nc, SSD_ROWS, SSD_CHUNK), F32),
            pltpu.VMEM((nc, SSD_ROWS, SSD_CHUNK), F32),
            pltpu.VMEM((nc, SSD_CHUNK, SSD_ROWS), F32),
            pltpu.VMEM((2, wx, n), F32),
        ],
        compiler_params=_cparams(2),
        name="ssd_mixer",
    )(*args)
    return (outs[0], outs[1]) if emit_state else (outs[0], None)


def _rope(x, cos, sin_signed, pair):
    n = x.shape[1]
    lane = lax.broadcasted_iota(jnp.int32, (1, n), 1)
    first = (lane & (2 * pair - 1)) < pair
    partner = jnp.where(first, pltpu.roll(x, n - pair, 1), pltpu.roll(x, pair, 1))
    return x * cos + partner * sin_signed


def _tile_lanes(x, k):
    return jnp.concatenate([x] * k, axis=1)


def _qkv_kernel(*refs, nct, has_pos, emit_cache):
    it = iter(refs)
    mla_ref, gqa_ref = next(it), next(it)
    qg_ref, kvg_ref, wuq_ref, wk_ref, wv_ref, gq_ref, gk_ref = (next(it) for _ in range(7))
    if has_pos:
        mcos_ref, msin_ref, gcos_ref, gsin_ref = (next(it) for _ in range(4))
    if nct:
        cckv_ref, ckr_ref, cgk_ref, cgv_ref = (next(it) for _ in range(4))
    qm_ref, km_ref, vm_ref, qg_out, kg_ref, vg_ref = (next(it) for _ in range(6))
    if emit_cache:
        ckv_out, kpe_out, kn_out, v_out = (next(it) for _ in range(4))

    t = pl.program_id(1)
    lane = lax.broadcasted_iota(jnp.int32, (1, LANES), 1)
    first_half = lane < GQA_HEAD_DIM
    mla_scale = math.log2(math.e) / math.sqrt(MLA_NOPE + MLA_ROPE)
    gqa_scale = math.log2(math.e) / math.sqrt(GQA_HEAD_DIM)
    ones_lane = jnp.where(lane == GQA_HEAD_DIM, 1.0, 0.0)

    def write_mla_kv(ckv, kpe_r):
        c16 = ckv.astype(BF16)
        km_ref[0] = (_dot(c16, wk_ref[...]) + _tile_lanes(kpe_r, MLA_HEADS)).astype(BF16)
        vm_ref[0] = (_dot(c16, wv_ref[...]) + _tile_lanes(ones_lane, MLA_HEADS)).astype(BF16)

    def new_tile():
        m = mla_ref[0]
        q_lat = m[:, 0:MLA_Q_LORA]
        kv_lat = m[:, MLA_Q_LORA:MLA_Q_LORA + MLA_KV_LORA]
        kpe = m[:, MLA_Q_LORA + MLA_KV_LORA:]
        qn = _rms(q_lat) * qg_ref[...]
        qm = _dot(qn.astype(BF16), wuq_ref[...])
        ckv = _rms(kv_lat) * kvg_ref[...]
        kpe_r = kpe
        if has_pos:
            mcos, msin = mcos_ref[...], msin_ref[...]
            qm = _rope(qm, _tile_lanes(mcos, MLA_HEADS), _tile_lanes(msin, MLA_HEADS), MLA_ROPE // 4)
            kpe_r = _rope(kpe, mcos, msin, MLA_ROPE // 4)
        qm_ref[0] = (qm * mla_scale).astype(BF16)
        write_mla_kv(ckv, kpe_r)

        g = gqa_ref[0]
        if has_pos:
            gcos, gsin = gcos_ref[...], gsin_ref[...]
        for h in range(GQA_HEADS):
            xh = g[:, h * LANES:(h + 1) * LANES]
            qh = _rms(xh, GQA_HEAD_DIM) * gq_ref[...]
            if has_pos:
                qh = _rope(qh, gcos, gsin, GQA_HEAD_DIM // 4)
            qg_out[0, :, h * LANES:(h + 1) * LANES] = (qh * gqa_scale).astype(BF16)
        kns = []
        for j in range(GQA_KV_HEADS):
            c0 = GQA_HEADS * LANES + j * LANES
            kn = _rms(g[:, c0:c0 + LANES]) * gk_ref[...]
            kns.append(kn)
            kr = _rope(kn, gcos, gsin, GQA_HEAD_DIM // 4) if has_pos else kn
            kg_ref[0, :, j * LANES:(j + 1) * LANES] = kr.astype(BF16)
        v0 = (GQA_HEADS + GQA_KV_HEADS) * LANES
        v = g[:, v0:v0 + GQA_KV_HEADS * LANES]
        for j in range(GQA_KV_HEADS):
            vj = jnp.where(first_half, v[:, j * LANES:(j + 1) * LANES], ones_lane)
            vg_ref[0, :, j * LANES:(j + 1) * LANES] = vj.astype(BF16)
        if emit_cache:
            ckv_out[0] = ckv
            kpe_out[0] = kpe[:, MLA_NOPE:MLA_NOPE + MLA_ROPE]
            kn_out[0] = jnp.where(first_half, kns[0], kns[1])
            v_out[0] = jnp.where(first_half, v[:, 0:LANES], v[:, LANES:2 * LANES])

    def ctx_tile():
        write_mla_kv(cckv_ref[0], ckr_ref[0])
        k = cgk_ref[0]
        k_sw = pltpu.roll(k, GQA_HEAD_DIM, 1)
        kg_ref[0, :, 0:LANES] = jnp.where(first_half, k, k_sw).astype(BF16)
        kg_ref[0, :, LANES:2 * LANES] = jnp.where(first_half, k_sw, k).astype(BF16)
        v = cgv_ref[0]
        vg_ref[0, :, 0:LANES] = jnp.where(first_half, v, ones_lane).astype(BF16)
        vg_ref[0, :, LANES:2 * LANES] = jnp.where(first_half, pltpu.roll(v, GQA_HEAD_DIM, 1), ones_lane).astype(BF16)

    if nct:
        pl.when(t < nct)(ctx_tile)
        pl.when(t >= nct)(new_tile)
    else:
        new_tile()


def _attention_operands(proj, wts, tables, ctx, layer, emit_cache, tm):
    b, l, _ = proj.shape
    has_pos = tables is not None
    nct = 0 if ctx is None else ctx[0].shape[2] // tm
    lk = l + nct * tm
    nt = lk // tm

    def new_t(t):
        return jnp.maximum(t - nct, 0) if nct else t

    in_specs = [
        pl.BlockSpec((1, tm, 512), lambda i, t: (i, new_t(t), COL_MLA // 512)),
        pl.BlockSpec((1, tm, 1024), lambda i, t: (i, new_t(t), COL_GQA // 1024)),
    ] + [_const_spec(w.shape) for w in wts]
    args = [proj, proj] + list(wts)
    if has_pos:
        in_specs += [pl.BlockSpec((tm, LANES), lambda i, t: (new_t(t), 0))] * 4
        args += list(tables)
    if nct:
        in_specs += [pl.BlockSpec((1, None, tm, LANES), lambda i, t: (i, layer, jnp.minimum(t, nct - 1), 0))] * 4
        args += list(ctx)
    q_spec = lambda w: pl.BlockSpec((1, tm, w), lambda i, t: (i, new_t(t), 0))
    k_spec = lambda w: pl.BlockSpec((1, tm, w), lambda i, t: (i, t, 0))
    out_specs = [q_spec(512), k_spec(512), k_spec(512), q_spec(512), k_spec(256), k_spec(256)]
    out_shape = [
        jax.ShapeDtypeStruct((b, l, 512), BF16), jax.ShapeDtypeStruct((b, lk, 512), BF16),
        jax.ShapeDtypeStruct((b, lk, 512), BF16), jax.ShapeDtypeStruct((b, l, 512), BF16),
        jax.ShapeDtypeStruct((b, lk, 256), BF16), jax.ShapeDtypeStruct((b, lk, 256), BF16),
    ]
    if emit_cache:
        out_specs += [q_spec(MLA_KV_LORA), q_spec(MLA_ROPE), q_spec(LANES), q_spec(LANES)]
        out_shape += [
            jax.ShapeDtypeStruct((b, l, MLA_KV_LORA), F32), jax.ShapeDtypeStruct((b, l, MLA_ROPE), F32),
            jax.ShapeDtypeStruct((b, l, LANES), F32), jax.ShapeDtypeStruct((b, l, LANES), F32),
        ]
    return pl.pallas_call(
        functools.partial(_qkv_kernel, nct=nct, has_pos=has_pos, emit_cache=emit_cache),
        grid=(b, nt),
        in_specs=in_specs,
        out_specs=out_specs,
        out_shape=out_shape,
        compiler_params=_cparams(2),
        name="attention_operands",
    )(*args)


def _attn_kernel(q_ref, k_ref, v_ref, o_ref, *, shared_kv):
    lane = lax.broadcasted_iota(jnp.int32, (1, LANES), 1)
    outs = []
    for a in range(2):
        cols = slice(0, LANES) if shared_kv else slice(a * LANES, (a + 1) * LANES)
        qa = q_ref[0, :, a * LANES:(a + 1) * LANES]
        s = lax.dot_general(qa, k_ref[0, :, cols], NT_DIMS, preferred_element_type=F32)
        p = jnp.exp2(s - jnp.max(s, axis=-1, keepdims=True)).astype(BF16)
        acc = _dot(p, v_ref[0, :, cols])
        outs.append(acc / acc[:, GQA_HEAD_DIM:GQA_HEAD_DIM + 1])
    o_ref[0] = jnp.where(lane < GQA_HEAD_DIM, outs[0], pltpu.roll(outs[1], GQA_HEAD_DIM, 1)).astype(BF16)


def _attention(q, k, v, shared_kv, tq):
    b, l, _ = q.shape
    lk = k.shape[1]
    kw = LANES if shared_kv else 2 * LANES
    return pl.pallas_call(
        functools.partial(_attn_kernel, shared_kv=shared_kv),
        grid=(b, 2, l // tq),
        in_specs=[
            pl.BlockSpec((1, tq, 2 * LANES), lambda i, j, t: (i, t, j)),
            pl.BlockSpec((1, lk, kw), lambda i, j, t: (i, 0, j)),
            pl.BlockSpec((1, lk, kw), lambda i, j, t: (i, 0, j)),
        ],
        out_specs=pl.BlockSpec((1, tq, LANES), lambda i, j, t: (i, t, j)),
        out_shape=jax.ShapeDtypeStruct((b, l, 2 * LANES), BF16),
        compiler_params=_cparams(3),
        name="attention_shared_kv" if shared_kv else "attention",
    )(q, k, v)


def _post_kernel(*refs, moe):
    it = iter(refs)
    ys_ref, z_ref, ym_ref, yg_ref, x_ref, g1_ref, sh2_ref, sc2_ref = (next(it) for _ in range(8))
    ng_ref, gpost_ref, gpre_ref, wout_ref = (next(it) for _ in range(4))
    wr_ref = next(it) if moe else None
    x1_ref, h2_ref = next(it), next(it)
    comb_ref, combt_ref = (next(it), next(it)) if moe else (None, None)

    y = _rms(ys_ref[0] * _silu(z_ref[0])) * ng_ref[...]
    cat = jnp.concatenate([y.astype(BF16), ym_ref[0], yg_ref[0]], axis=1)
    mix = _dot(cat, wout_ref[...])
    x1 = x_ref[0] + g1_ref[0] * (_rms(mix) * gpost_ref[...])
    x1_ref[0] = x1
    h2 = _rms(x1) * gpre_ref[...] * (1.0 + sc2_ref[0]) + sh2_ref[0]
    h16 = h2.astype(BF16)
    h2_ref[0] = h16
    if moe:
        w_hi = wr_ref[...].astype(BF16)
        w_lo = (wr_ref[...] - w_hi.astype(F32)).astype(BF16)
        logits = _dot(h16, w_hi) + _dot(h16, w_lo)
        lane = lax.broadcasted_iota(jnp.int32, (1, LANES), 1).astype(F32)
        lg = jnp.where(lane < N_EXPERTS, logits, -jnp.inf)
        m1 = jnp.max(lg, axis=-1, keepdims=True)
        i1 = jnp.min(jnp.where(lg == m1, lane, float(LANES)), axis=-1, keepdims=True)
        lg2 = jnp.where(lane == i1, -jnp.inf, lg)
        m2 = jnp.max(lg2, axis=-1, keepdims=True)
        i2 = jnp.min(jnp.where(lg2 == m2, lane, float(LANES)), axis=-1, keepdims=True)
        e = jnp.exp(m2 - m1)
        comb = jnp.where(lane == i1, 1.0 / (1.0 + e), jnp.where(lane == i2, e / (1.0 + e), 0.0))
        comb_ref[0] = comb
        combt_ref[...] = comb.T[0:N_EXPERTS, :]


def _post_attention(yssd, proj, ym, yg, x, mod, mod_row, ng, gpost, gpre, w_out, w_router, tm):
    b, l, d = x.shape
    moe = w_router is not None
    tok = lambda w: pl.BlockSpec((1, tm, w), lambda i, t: (i, t, 0))
    modk = lambda k: pl.BlockSpec((1, 1, d), lambda i, t: (mod_row(i), 0, k))
    in_specs = [tok(SSD_INNER), tok(SSD_INNER), tok(256), tok(256), tok(d), modk(2), modk(3), modk(4),
                _const_spec(ng.shape), _const_spec(gpost.shape), _const_spec(gpre.shape),
                _const_spec(w_out.shape)]
    args = [yssd, proj, ym, yg, x, mod, mod, mod, ng, gpost, gpre, w_out]
    out_specs = [tok(d), tok(d)]
    out_shape = [jax.ShapeDtypeStruct((b, l, d), F32), jax.ShapeDtypeStruct((b, l, d), BF16)]
    if moe:
        in_specs.append(_const_spec(w_router.shape))
        args.append(w_router)
        out_specs += [tok(LANES), pl.BlockSpec((N_EXPERTS, tm), lambda i, t: (0, i * (l // tm) + t))]
        out_shape += [jax.ShapeDtypeStruct((b, l, LANES), F32), jax.ShapeDtypeStruct((N_EXPERTS, b * l), F32)]
    outs = pl.pallas_call(
        functools.partial(_post_kernel, moe=moe),
        grid=(b, l // tm),
        in_specs=in_specs,
        out_specs=out_specs,
        out_shape=out_shape,
        compiler_params=_cparams(2),
        name="post_attention",
    )(*args)
    return outs if moe else (outs[0], outs[1], None, None)


def _ffn_kernel(h_ref, x1_ref, g2_ref, gp_ref, wg_ref, wu_ref, wd_ref, o_ref):
    h = h_ref[0]
    act = (_silu(_dot(h, wg_ref[...])) * _dot(h, wu_ref[...])).astype(BF16)
    f = _dot(act, wd_ref[...])
    o_ref[0] = x1_ref[0] + g2_ref[0] * (_rms(f) * gp_ref[...])


def _dense_ffn(h2, x1, mod, mod_row, gp, wg, wu, wd, tm):
    b, l, d = x1.shape
    tok = pl.BlockSpec((1, tm, d), lambda i, t: (i, t, 0))
    return pl.pallas_call(
        _ffn_kernel,
        grid=(b, l // tm),
        in_specs=[tok, tok, pl.BlockSpec((1, 1, d), lambda i, t: (mod_row(i), 0, 5)),
                  _const_spec(gp.shape), _const_spec(wg.shape), _const_spec(wu.shape), _const_spec(wd.shape)],
        out_specs=tok,
        out_shape=jax.ShapeDtypeStruct((b, l, d), F32),
        compiler_params=_cparams(2),
        name="dense_ffn",
    )(h2, x1, mod, gp, wg, wu, wd)


MOE_BLOCK = 512
MOE_CAP = 160
MOE_SUPER = 2048


def _moe_kernel(h_ref, comb_ref, combt_ref, x1_ref, g2_ref, gp_ref, wg_ref, wu_ref, wd_ref, o_ref,
                acc_s, pos_s):
    e = pl.program_id(1)
    sub = pl.program_id(2)
    tb, cap = MOE_BLOCK, MOE_CAP
    off = pl.multiple_of(sub * tb, tb)

    @pl.when(e == 0)
    def _():
        acc_s[pl.ds(off, tb), :] = jnp.zeros((tb, acc_s.shape[1]), F32)
        si = lax.broadcasted_iota(jnp.int32, (tb, tb), 0)
        ti = lax.broadcasted_iota(jnp.int32, (tb, tb), 1)
        before = jnp.where(si < ti, 1.0, 0.0).astype(BF16)
        sel = jnp.where(combt_ref[:, pl.ds(off, tb)] > 0.0, 1.0, 0.0).astype(BF16)
        pos_s[:, pl.ds(off, tb)] = _dot(sel, before)

    sel_e = combt_ref[pl.ds(e, 1), pl.ds(off, tb)] > 0.0
    pos_e = pos_s[pl.ds(e, 1), pl.ds(off, tb)]
    n_e = jnp.sum(jnp.where(sel_e, 1.0, 0.0)).astype(jnp.int32)
    lane = lax.broadcasted_iota(jnp.int32, (1, LANES), 1)
    gate = jnp.sum(jnp.where(lane == e, comb_ref[pl.ds(off, tb), :], 0.0), axis=-1, keepdims=True)
    h = h_ref[pl.ds(off, tb), :]
    slot = lax.broadcasted_iota(jnp.int32, (cap, 1), 0).astype(F32)

    def tile(j, carry):
        first = (j * cap).astype(F32)
        pick = jnp.where(sel_e & (pos_e - first == slot), 1.0, 0.0).astype(BF16)
        xg = _dot(pick, h).astype(BF16)
        act = (_silu(_dot(xg, wg_ref[0])) * _dot(xg, wu_ref[0])).astype(BF16)
        y = _dot(act, wd_ref[0]).astype(BF16)
        back = lax.dot_general(pick, y, TN_DIMS, preferred_element_type=F32)
        acc_s[pl.ds(off, tb), :] += gate * back
        return carry

    lax.fori_loop(0, (n_e + cap - 1) // cap, tile, 0)

    @pl.when(e == pl.num_programs(1) - 1)
    def _():
        o_ref[...] = x1_ref[...] + g2_ref[0] * (_rms(acc_s[pl.ds(off, tb), :]) * gp_ref[...])


def _moe_ffn(h2, comb, combt, x1, mod, mod_row, gp, wg, wu, wd, sup):
    t, d = x1.shape
    n_e, _, ff = wg.shape
    tb = MOE_BLOCK
    nsub = sup // tb
    once = pl.Buffered(1)
    last_only = lambda s, e, u: (s * nsub + jnp.where(e == n_e - 1, u, 0), 0)
    return pl.pallas_call(
        _moe_kernel,
        grid=(t // sup, n_e, nsub),
        in_specs=[pl.BlockSpec((sup, d), lambda s, e, u: (s, 0), pipeline_mode=once),
                  pl.BlockSpec((sup, LANES), lambda s, e, u: (s, 0), pipeline_mode=once),
                  pl.BlockSpec((N_EXPERTS, sup), lambda s, e, u: (0, s), pipeline_mode=once),
                  pl.BlockSpec((tb, d), last_only, pipeline_mode=once),
                  pl.BlockSpec((1, 1, d), lambda s, e, u: (mod_row(s), 0, 5)),
                  pl.BlockSpec((1, d), lambda s, e, u: (0, 0)),
                  pl.BlockSpec((1, d, ff), lambda s, e, u: (e, 0, 0)),
                  pl.BlockSpec((1, d, ff), lambda s, e, u: (e, 0, 0)),
                  pl.BlockSpec((1, ff, d), lambda s, e, u: (e, 0, 0))],
        out_specs=pl.BlockSpec((tb, d), last_only),
        out_shape=jax.ShapeDtypeStruct((t, d), F32),
        scratch_shapes=[pltpu.VMEM((sup, d), F32), pltpu.VMEM((N_EXPERTS, sup), F32)],
        compiler_params=_cparams(3),
        name="moe_ffn",
    )(h2, comb, combt, x1, mod, gp, wg, wu, wd)


def _in_proj_columns():
    o_dt = 1536
    o_ql, o_kv, o_kpe = 1552, 1808, 1936
    o_gq, o_gk, o_gv = 1968, 2224, 2352
    idx = np.full((IN_COLS_PADDED,), -1, np.int64)
    idx[0:1536] = np.arange(1536)
    idx[COL_MLA:COL_MLA + 256] = o_ql + np.arange(256)
    idx[COL_MLA + 256:COL_MLA + 384] = o_kv + np.arange(128)
    idx[COL_MLA + 384 + MLA_NOPE:COL_MLA + 384 + MLA_NOPE + MLA_ROPE] = o_kpe + np.arange(MLA_ROPE)
    for h in range(GQA_HEADS):
        idx[COL_GQA + h * LANES:COL_GQA + h * LANES + 64] = o_gq + h * 64 + np.arange(64)
    for j in range(GQA_KV_HEADS):
        for r in range(2):
            c0 = COL_GQA + 512 + j * LANES + r * 64
            idx[c0:c0 + 64] = o_gk + j * 64 + np.arange(64)
            idx[c0 + 256:c0 + 256 + 64] = o_gv + j * 64 + np.arange(64)
    for g in range(SSD_GROUPS):
        for d in range(2):
            c0 = COL_DT + g * LANES + d * HEADS_PER_GROUP
            idx[c0:c0 + HEADS_PER_GROUP] = o_dt + d * SSD_HEADS + g * HEADS_PER_GROUP + np.arange(HEADS_PER_GROUP)
    return idx


def _gather_cols(w, idx):
    cols = jnp.take(w, jnp.asarray(np.maximum(idx, 0)), axis=1)
    return jnp.where(jnp.asarray(idx >= 0)[None, :], cols, 0.0)


def _mla_weight_columns():
    per_q = MLA_NOPE + MLA_ROPE
    uq = np.full((MLA_HEADS * LANES,), -1, np.int64)
    uk = np.full((MLA_HEADS * LANES,), -1, np.int64)
    uv = np.full((MLA_HEADS * LANES,), -1, np.int64)
    for h in range(MLA_HEADS):
        uq[h * LANES:h * LANES + per_q] = h * per_q + np.arange(per_q)
        uk[h * LANES:h * LANES + MLA_NOPE] = h * (MLA_NOPE + MLA_V) + np.arange(MLA_NOPE)
        uv[h * LANES:h * LANES + MLA_V] = h * (MLA_NOPE + MLA_V) + MLA_NOPE + np.arange(MLA_V)
    return uq, uk, uv


def _rope_tables(n_tokens):
    t = np.arange(n_tokens)
    row, colp = (t // GRID_W).astype(np.float64), (t % GRID_W).astype(np.float64)

    def table(width, lane0, head_dim, reps):
        cos = np.ones((n_tokens, width))
        sin = np.zeros((n_tokens, width))
        half = head_dim // 2
        quarter = half // 2
        for dim in range(head_dim):
            pos = row if dim < half else colp
            inv = ROPE_BASE ** (-(2.0 * (dim % quarter)) / half)
            ang = pos * inv
            sign = -1.0 if (dim % half) < quarter else 1.0
            for r in range(reps):
                cos[:, lane0 + r * head_dim + dim] = np.cos(ang)
                sin[:, lane0 + r * head_dim + dim] = sign * np.sin(ang)
        return jnp.asarray(cos, F32), jnp.asarray(sin, F32)

    mcos, msin = table(LANES, MLA_NOPE, MLA_ROPE, 1)
    gcos, gsin = table(LANES, 0, GQA_HEAD_DIM, 2)
    return mcos, msin, gcos, gsin


def _trunk_layer(i, x, mod_i, mod_row, p, tables, ctx):
    b, l, d = x.shape
    tm = min(512, l)
    emit = ctx is None
    proj = _in_projection(x, mod_i, mod_row, p["g_mix_pre"][i], p["w_in"][i], tm)
    h0 = None if ctx is None else ctx["state"][:, i]
    yssd, state = _ssd_mixer(proj, p["conv_w"][i], p["conv_b"][i], p["ssd_par"][i], h0, emit)
    ctx_kv = None if ctx is None else ctx["kv"]
    ops = _attention_operands(proj, p["attn_w"][i], tables, ctx_kv, i, emit, tm)
    qm, km, vm, qg, kg, vg = ops[:6]
    tq = min(256, l)
    ym = _attention(qm, km, vm, False, tq)
    yg = _attention(qg, kg, vg, True, tq)
    moe = i % 2 == 1
    j = i // 2
    x1, h2, comb, combt = _post_attention(
        yssd, proj, ym, yg, x, mod_i, mod_row, p["ssd_norm_g"][i], p["g_mix_post"][i], p["g_ffn_pre"][i],
        p["w_out"][i], p["w_router"][j] if moe else None, tm)
    if moe:
        per_batch = ctx is not None
        sup = min(MOE_SUPER, l if per_batch else b * l)
        sup_row = (lambda s: mod_row(s // (l // sup))) if per_batch else mod_row
        x2 = _moe_ffn(h2.reshape(b * l, d), comb.reshape(b * l, LANES), combt, x1.reshape(b * l, d), mod_i,
                      sup_row, p["g_ffn_post"][i], p["moe_wg"][j], p["moe_wu"][j], p["moe_wd"][j], sup)
        x2 = x2.reshape(b, l, d)
    else:
        x2 = _dense_ffn(h2, x1, mod_i, mod_row, p["g_ffn_post"][i],
                        p["ffn_wg"][j], p["ffn_wu"][j], p["ffn_wd"][j], tm)
    return x2, (state,) + tuple(ops[6:])


def kernel(x_prompt, x_sample, state_ssd, cache_mla_ckv, cache_mla_krope, cache_gqa_k, cache_gqa_v, c, c_ctx, w_mod, b_mod, g_mix_pre, g_mix_post, g_ffn_pre, g_ffn_post, w_in, ssd_conv_w, ssd_conv_b, ssd_A_log, ssd_dt_bias, ssd_D, ssd_norm_g, mla_q_norm_g, mla_w_uq, mla_kv_norm_g, mla_w_ukv, gqa_q_norm_g, gqa_k_norm_g, w_out, ffn_w_gate, ffn_w_up, ffn_w_down, moe_w_router, moe_w_gate, moe_w_up, moe_w_down):
    depth = w_in.shape[0]
    d = x_prompt.shape[-1]
    n_dec = x_sample.shape[0]
    hpg = HEADS_PER_GROUP

    in_idx = _in_proj_columns()
    uq_idx, uk_idx, uv_idx = _mla_weight_columns()
    row2 = lambda a: a.reshape(depth, 1, a.shape[-1])
    def ssd_rows(fwd, bwd):
        cols = [fwd.reshape(depth, SSD_GROUPS, hpg), bwd.reshape(depth, SSD_GROUPS, hpg),
                jnp.zeros((depth, SSD_GROUPS, SSD_ROWS - 2 * hpg), F32)]
        return jnp.concatenate(cols, axis=-1)

    par = jnp.concatenate([ssd_rows(ssd_dt_bias[:, 0], ssd_dt_bias[:, 1]),
                           ssd_rows(ssd_A_log[:, 0], ssd_A_log[:, 1]),
                           ssd_rows(ssd_D, jnp.zeros_like(ssd_D))], axis=-1)
    par = jnp.broadcast_to(par[..., None], par.shape + (LANES,))
    zeros64 = jnp.zeros((depth, 1, GQA_HEAD_DIM), F32)
    attn_w = []
    for i in range(depth):
        attn_w.append((
            mla_q_norm_g[i][None, :], mla_kv_norm_g[i][None, :],
            _gather_cols(mla_w_uq[i], uq_idx).astype(BF16),
            _gather_cols(mla_w_ukv[i], uk_idx).astype(BF16),
            _gather_cols(mla_w_ukv[i], uv_idx).astype(BF16),
            jnp.concatenate([gqa_q_norm_g[i][None, :], zeros64[i]], axis=1),
            jnp.concatenate([gqa_k_norm_g[i][None, :]] * 2, axis=1),
        ))
    p = dict(
        g_mix_pre=row2(g_mix_pre), g_mix_post=row2(g_mix_post), g_ffn_pre=row2(g_ffn_pre),
        g_ffn_post=row2(g_ffn_post), ssd_norm_g=row2(ssd_norm_g),
        w_in=[_gather_cols(w_in[i], in_idx).astype(BF16) for i in range(depth)],
        conv_w=ssd_conv_w, conv_b=row2(ssd_conv_b), ssd_par=par, attn_w=attn_w,
        w_out=w_out.astype(BF16),
        ffn_wg=ffn_w_gate.astype(BF16), ffn_wu=ffn_w_up.astype(BF16), ffn_wd=ffn_w_down.astype(BF16),
        w_router=jnp.pad(moe_w_router, ((0, 0), (0, 0), (0, LANES - N_EXPERTS))),
        moe_wg=moe_w_gate.astype(BF16), moe_wu=moe_w_up.astype(BF16), moe_wd=moe_w_down.astype(BF16),
    )

    rows = -(-(1 + n_dec) // 8) * 8
    c_all = jnp.concatenate([c_ctx[None, :], c, jnp.zeros((rows - 1 - n_dec, d), F32)], axis=0)
    mod = _modulation(c_all, w_mod, b_mod).reshape(depth, rows, 1, 6 * d)

    y = x_prompt
    collected = [[], [], [], [], []]
    for i in range(depth):
        y, outs = _trunk_layer(i, y, mod[i], lambda bi: 0, p, None, None)
        for lst, t in zip(collected, outs):
            lst.append(t)
    y_prompt = y
    bsz, seq = x_prompt.shape[:2]
    new_state = jnp.stack(collected[0], axis=1)
    new_ckv = jnp.stack(collected[1], axis=1)
    new_krope = jnp.stack(collected[2], axis=1)
    new_k = jnp.stack(collected[3], axis=1).reshape(bsz, depth, seq, GQA_KV_HEADS, GQA_HEAD_DIM)
    new_v = jnp.stack(collected[4], axis=1).reshape(bsz, depth, seq, GQA_KV_HEADS, GQA_HEAD_DIM)

    past = cache_mla_ckv.shape[2]
    krope_blk = jnp.pad(cache_mla_krope, ((0, 0), (0, 0), (0, 0), (MLA_NOPE, LANES - MLA_NOPE - MLA_ROPE)))
    ctx = dict(
        state=state_ssd,
        kv=(cache_mla_ckv, krope_blk,
            cache_gqa_k.reshape(n_dec, depth, past, LANES), cache_gqa_v.reshape(n_dec, depth, past, LANES)),
    )
    tables = _rope_tables(x_sample.shape[1])
    y = x_sample
    for i in range(depth):
        y, _ = _trunk_layer(i, y, mod[i], lambda bi: bi + 1, p, tables, ctx)
    return (y_prompt, y, new_state, new_ckv, new_krope, new_k, new_v)
```

```python
import functools
import itertools
import math

import numpy as np
import jax
import jax.numpy as jnp
from jax import lax
from jax.experimental import pallas as pl
from jax.experimental.pallas import tpu as pltpu

F32 = jnp.float32
BF16 = jnp.bfloat16

EPS = 1e-6
ROPE_BASE = 10000.0
GRID_W = 64
SSD_HEAD_DIM = 64
SSD_HEADS = 8
SSD_GROUPS = 2
SSD_STATE = 128
SSD_CHUNK = 128
SSD_INNER = SSD_HEADS * SSD_HEAD_DIM
HEADS_PER_GROUP = SSD_HEADS // SSD_GROUPS
SSD_ROWS = 16
MLA_HEADS = 4
MLA_V = 64
MLA_NOPE = 64
MLA_ROPE = 32
MLA_Q_LORA = 256
MLA_KV_LORA = 128
GQA_HEADS = 4
GQA_KV_HEADS = 2
GQA_HEAD_DIM = 64
N_EXPERTS = 8

LANES = 128

COL_Z = 0
COL_X = 512
COL_B = 1024
COL_C = 1280
COL_MLA = 1536
COL_GQA = 2048
COL_DT = 3072
IN_COLS_PADDED = 3328

VMEM_LIMIT = 56 * 1024 * 1024

NT_DIMS = (((1,), (1,)), ((), ()))
TN_DIMS = (((0,), (0,)), ((), ()))


def _cparams(n_grid):
    return pltpu.CompilerParams(
        dimension_semantics=("arbitrary",) * n_grid, vmem_limit_bytes=VMEM_LIMIT)


def _const_spec(shape):
    nd = len(shape)
    return pl.BlockSpec(shape, lambda *_: (0,) * nd, pipeline_mode=pl.Buffered(1))


def _dot(a, b):
    return jnp.dot(a, b, preferred_element_type=F32)


def _rms(x, width=None):
    n = x.shape[-1] if width is None else width
    return x * lax.rsqrt(jnp.sum(x * x, axis=-1, keepdims=True) * (1.0 / n) + EPS)


def _silu(x):
    return x * jax.nn.sigmoid(x)


def _split3(v):
    hi = v.astype(BF16)
    r = v - hi.astype(F32)
    mid = r.astype(BF16)
    lo = (r - mid.astype(F32)).astype(BF16)
    return hi, mid, lo


def _mod_kernel(c_ref, w_ref, b_ref, o_ref):
    s = _silu(c_ref[...]).astype(BF16)
    o_ref[0] = _dot(s, w_ref[0].astype(BF16)) + b_ref[0]


def _modulation(c_all, w_mod, b_mod):
    depth, d, n = w_mod.shape
    tn = 1536
    rows = c_all.shape[0]
    return pl.pallas_call(
        _mod_kernel,
        grid=(depth, n // tn),
        in_specs=[
            pl.BlockSpec((rows, d), lambda i, j: (0, 0)),
            pl.BlockSpec((1, d, tn), lambda i, j: (i, 0, j)),
            pl.BlockSpec((1, 1, tn), lambda i, j: (i, 0, j)),
        ],
        out_specs=pl.BlockSpec((1, rows, tn), lambda i, j: (i, 0, j)),
        out_shape=jax.ShapeDtypeStruct((depth, rows, n), F32),
        compiler_params=_cparams(2),
        name="modulation",
    )(c_all, w_mod, b_mod.reshape(depth, 1, n))


def _inproj_kernel(x_ref, sh_ref, sc_ref, g_ref, w_ref, o_ref):
    h = _rms(x_ref[0]) * g_ref[...]
    h = h * (1.0 + sc_ref[0]) + sh_ref[0]
    o_ref[0] = _dot(h.astype(BF16), w_ref[...])


def _in_projection(x, mod, mod_row, gain, w_in_p, tm):
    b, l, d = x.shape
    n = w_in_p.shape[1]
    return pl.pallas_call(
        _inproj_kernel,
        grid=(b, l // tm),
        in_specs=[
            pl.BlockSpec((1, tm, d), lambda i, t: (i, t, 0)),
            pl.BlockSpec((1, 1, d), lambda i, t: (mod_row(i), 0, 0)),
            pl.BlockSpec((1, 1, d), lambda i, t: (mod_row(i), 0, 1)),
            _const_spec((1, d)),
            _const_spec((d, n)),
        ],
        out_specs=pl.BlockSpec((1, tm, n), lambda i, t: (i, t, 0)),
        out_shape=jax.ShapeDtypeStruct((b, l, n), F32),
        compiler_params=_cparams(2),
        name="in_projection",
    )(x, mod, mod, gain, w_in_p)


def _ssd_kernel(*refs, nc, seq, has_h0, emit_state):
    it = iter(refs)
    x_ref, b_ref, c_ref, dt_ref = next(it), next(it), next(it), next(it)
    cwx_ref, cwb_ref, cwc_ref = next(it), next(it), next(it)
    cbx_ref, cbb_ref, cbc_ref = next(it), next(it), next(it)
    par_ref = next(it)
    h0_ref = next(it) if has_h0 else None
    y_ref = next(it)
    st_ref = next(it) if emit_state else None
    xt_s, yd_s, bc_s, dtv_s, arow_s, acol_s, h_s = (next(it) for _ in range(7))

    q = SSD_CHUNK
    hpg = HEADS_PER_GROUP
    hd = SSD_HEAD_DIM
    wx = hpg * hd
    r = SSD_ROWS
    row = lax.broadcasted_iota(jnp.int32, (q, 1), 0)
    si = lax.broadcasted_iota(jnp.int32, (q, q), 0)
    ti = lax.broadcasted_iota(jnp.int32, (q, q), 1)
    le = si <= ti
    ge = si >= ti
    tri_le = jnp.where(le, 1.0, 0.0).astype(BF16)
    tri_ge = jnp.where(ge, 1.0, 0.0).astype(BF16)
    par = par_ref[0]
    bias_t, alog_t, d_t = par[0:r], par[r:2 * r], par[2 * r:3 * r]

    for d in range(2):
        if has_h0:
            h_s[d] = h0_ref[0, d].reshape(wx, SSD_STATE)
        else:
            h_s[d] = jnp.zeros((wx, SSD_STATE), F32)

    def conv_silu(ref, w_ref, bias_ref, c):
        off = pl.multiple_of(c * q, q)
        u = ref[0, pl.ds(off, q), :]
        prev = ref[0, pl.ds(jnp.maximum(off - 1, 0), 1), :] * jnp.where(c > 0, 1.0, 0.0)
        nxt = ref[0, pl.ds(jnp.minimum(off + q, seq - 1), 1), :] * jnp.where(c < nc - 1, 1.0, 0.0)
        up = jnp.where(row == 0, prev, pltpu.roll(u, 1, 0))
        un = jnp.where(row == q - 1, nxt, pltpu.roll(u, q - 1, 0))
        w = w_ref[...]
        return _silu(up * w[0:1] + u * w[1:2] + un * w[2:3] + bias_ref[...])

    tri_rows = jnp.concatenate([tri_le, tri_ge], axis=1)
    tri_cols = jnp.concatenate([tri_ge, tri_le], axis=0)
    fwd_row = lax.broadcasted_iota(jnp.int32, (r, 1), 0) < hpg
    fwd_lane = lax.broadcasted_iota(jnp.int32, (1, r), 1) < hpg

    def prep_body(c, carry):
        off = pl.multiple_of(c * q, q)
        xt = conv_silu(x_ref, cwx_ref, cbx_ref, c).T
        xt_s[c] = xt
        bc_s[pl.ds(off, q), 0:SSD_STATE] = conv_silu(b_ref, cwb_ref, cbb_ref, c).astype(BF16)
        bc_s[pl.ds(off, q), SSD_STATE:2 * SSD_STATE] = conv_silu(c_ref, cwc_ref, cbc_ref, c).astype(BF16)
        dtr = dt_ref[0, pl.ds(off, q), :].T[0:r, :] + bias_t
        dtv = jnp.maximum(dtr, 0.0) + jnp.log1p(jnp.exp(-jnp.abs(dtr)))
        dtv_s[c] = dtv
        parts = _split3(dtv * (-jnp.exp(alog_t)))
        rows = sum(_dot(part, tri_rows) for part in parts)
        cols = sum(lax.dot_general(tri_cols, part, NT_DIMS, preferred_element_type=F32) for part in parts)
        arow_s[c] = jnp.where(fwd_row, rows[:, 0:q], rows[:, q:2 * q])
        acol_s[c] = jnp.where(fwd_lane, cols[0:q, :], cols[q:2 * q, :])
        return carry

    def chunk(c, d):
        off = pl.multiple_of(c * q, q)
        base, mask = (0, le) if d == 0 else (hpg, ge)
        xt = xt_s[c]
        b16 = bc_s[pl.ds(off, q), 0:SSD_STATE]
        c16 = bc_s[pl.ds(off, q), SSD_STATE:2 * SSD_STATE]
        dtv, a_rows, a_cols = dtv_s[c], arow_s[c], acol_s[c]
        tot = a_rows[:, q - 1:q] if d == 0 else a_rows[:, 0:1]
        st = lax.dot_general(b16, c16, NT_DIMS, preferred_element_type=F32)
        hs = h_s[d]
        yoff = lax.dot_general(hs.astype(BF16), c16, NT_DIMS, preferred_element_type=F32)
        yield
        outs, xdecs, hnew = [], [], []
        for hl in range(hpg):
            k = base + hl
            rs = slice(hl * hd, (hl + 1) * hd)
            a_t = a_rows[k:k + 1, :]
            tot_k = tot[k:k + 1, :]
            seg = a_t - a_cols[:, k:k + 1]
            w = (st * jnp.exp(jnp.where(mask, seg, -jnp.inf))).astype(BF16)
            xdt = xt[rs, :] * dtv[k:k + 1, :]
            outs.append(_dot(xdt.astype(BF16), w) + yoff[rs, :] * jnp.exp(a_t))
            xdecs.append((xdt * jnp.exp(tot_k - a_t)).astype(BF16))
            hnew.append(hs[rs, :] * jnp.exp(tot_k))
            yield
        h_s[d] = jnp.concatenate(hnew, axis=0) + _dot(jnp.concatenate(xdecs, axis=0), b16)
        yd_s[d, c] = jnp.concatenate(outs, axis=0)

    def scan_body(i, carry):
        for _ in itertools.zip_longest(chunk(i, 0), chunk(nc - 1 - i, 1)):
            pass
        return carry

    def out_body(c, carry):
        xt = xt_s[c]
        skip = [xt[hl * hd:(hl + 1) * hd, :] * d_t[hl:hl + 1, :] for hl in range(hpg)]
        yt = yd_s[0, c] + yd_s[1, c] + jnp.concatenate(skip, axis=0)
        y_ref[0, pl.ds(pl.multiple_of(c * q, q), q), :] = yt.T
        return carry

    lax.fori_loop(0, nc, prep_body, 0, unroll=2)
    lax.fori_loop(0, nc, scan_body, 0, unroll=2)
    lax.fori_loop(0, nc, out_body, 0, unroll=4)
    if emit_state:
        for d in range(2):
            st_ref[0, d] = h_s[d].reshape(hpg, SSD_HEAD_DIM, SSD_STATE)


def _ssd_mixer(proj, conv_w, conv_b, par, h0, emit_state):
    b, l, _ = proj.shape
    g = SSD_GROUPS
    wx = HEADS_PER_GROUP * SSD_HEAD_DIM
    n = SSD_STATE
    nc = l // SSD_CHUNK
    has_h0 = h0 is not None

    def col(width, start):
        blk = start // width
        return pl.BlockSpec((1, l, width), lambda i, j: (i, 0, blk + j))

    def cw(width, start, rows):
        blk = start // width
        return pl.BlockSpec((rows, width), lambda i, j: (0, blk + j))

    in_specs = [
        col(wx, COL_X), col(n, COL_B), col(n, COL_C), col(LANES, COL_DT),
        cw(wx, 0, 3), cw(n, SSD_INNER, 3), cw(n, SSD_INNER + g * n, 3),
        cw(wx, 0, 1), cw(n, SSD_INNER, 1), cw(n, SSD_INNER + g * n, 1),
        pl.BlockSpec((1, 3 * SSD_ROWS, LANES), lambda i, j: (j, 0, 0)),
    ]
    args = [proj, proj, proj, proj, conv_w, conv_w, conv_w, conv_b, conv_b, conv_b, par]
    state_spec = pl.BlockSpec((1, 2, HEADS_PER_GROUP, SSD_HEAD_DIM, n), lambda i, j: (i, 0, j, 0, 0))
    if has_h0:
        in_specs.append(state_spec)
        args.append(h0)
    out_specs = [pl.BlockSpec((1, l, wx), lambda i, j: (i, 0, j))]
    out_shape = [jax.ShapeDtypeStruct((b, l, SSD_INNER), F32)]
    if emit_state:
        out_specs.append(state_spec)
        out_shape.append(jax.ShapeDtypeStruct((b, 2, SSD_HEADS, SSD_HEAD_DIM, n), F32))
    outs = pl.pallas_call(
        functools.partial(_ssd_kernel, nc=nc, seq=l, has_h0=has_h0, emit_state=emit_state),
        grid=(b, g),
        in_specs=in_specs,
        out_specs=out_specs,
        out_shape=out_shape,
        scratch_shapes=[
            pltpu.VMEM((nc, wx, SSD_CHUNK), F32),
            pltpu.VMEM((2, nc, wx, SSD_CHUNK), F32),
            pltpu.VMEM((l, 2 * n), BF16),
            pltpu.VMEM((nc, SSD_ROWS, SSD_CHUNK), F32),
            pltpu.VMEM((nc, SSD_ROWS, SSD_CHUNK), F32),
            pltpu.VMEM((nc, SSD_CHUNK, SSD_ROWS), F32),
            pltpu.VMEM((2, wx, n), F32),
        ],
        compiler_params=_cparams(2),
        name="ssd_mixer",
    )(*args)
    return (outs[0], outs[1]) if emit_state else (outs[0], None)


def _rope(x, cos, sin_signed, pair):
    n = x.shape[1]
    lane = lax.broadcasted_iota(jnp.int32, (1, n), 1)
    first = (lane & (2 * pair - 1)) < pair
    partner = jnp.where(first, pltpu.roll(x, n - pair, 1), pltpu.roll(x, pair, 1))
    return x * cos + partner * sin_signed


def _tile_lanes(x, k):
    return jnp.concatenate([x] * k, axis=1)


def _qkv_kernel(*refs, nct, has_pos, emit_cache):
    it = iter(refs)
    mla_ref, gqa_ref = next(it), next(it)
    qg_ref, kvg_ref, wuq_ref, wk_ref, wv_ref, gq_ref, gk_ref = (next(it) for _ in range(7))
    if has_pos:
        mcos_ref, msin_ref, gcos_ref, gsin_ref = (next(it) for _ in range(4))
    if nct:
        cckv_ref, ckr_ref, cgk_ref, cgv_ref = (next(it) for _ in range(4))
    qm_ref, km_ref, vm_ref, qg_out, kg_ref, vg_ref = (next(it) for _ in range(6))
    if emit_cache:
        ckv_out, kpe_out, kn_out, v_out = (next(it) for _ in range(4))

    t = pl.program_id(1)
    lane = lax.broadcasted_iota(jnp.int32, (1, LANES), 1)
    first_half = lane < GQA_HEAD_DIM
    mla_scale = math.log2(math.e) / math.sqrt(MLA_NOPE + MLA_ROPE)
    gqa_scale = math.log2(math.e) / math.sqrt(GQA_HEAD_DIM)
    ones_lane = jnp.where(lane == GQA_HEAD_DIM, 1.0, 0.0)

    def write_mla_kv(ckv, kpe_r):
        c16 = ckv.astype(BF16)
        km_ref[0] = (_dot(c16, wk_ref[...]) + _tile_lanes(kpe_r, MLA_HEADS)).astype(BF16)
        vm_ref[0] = (_dot(c16, wv_ref[...]) + _tile_lanes(ones_lane, MLA_HEADS)).astype(BF16)

    def new_tile():
        m = mla_ref[0]
        q_lat = m[:, 0:MLA_Q_LORA]
        kv_lat = m[:, MLA_Q_LORA:MLA_Q_LORA + MLA_KV_LORA]
        kpe = m[:, MLA_Q_LORA + MLA_KV_LORA:]
        qn = _rms(q_lat) * qg_ref[...]
        qm = _dot(qn.astype(BF16), wuq_ref[...])
        ckv = _rms(kv_lat) * kvg_ref[...]
        kpe_r = kpe
        if has_pos:
            mcos, msin = mcos_ref[...], msin_ref[...]
            qm = _rope(qm, _tile_lanes(mcos, MLA_HEADS), _tile_lanes(msin, MLA_HEADS), MLA_ROPE // 4)
            kpe_r = _rope(kpe, mcos, msin, MLA_ROPE // 4)
        qm_ref[0] = (qm * mla_scale).astype(BF16)
        write_mla_kv(ckv, kpe_r)

        g = gqa_ref[0]
        if has_pos:
            gcos, gsin = gcos_ref[...], gsin_ref[...]
        for h in range(GQA_HEADS):
            xh = g[:, h * LANES:(h + 1) * LANES]
            qh = _rms(xh, GQA_HEAD_DIM) * gq_ref[...]
            if has_pos:
                qh = _rope(qh, gcos, gsin, GQA_HEAD_DIM // 4)
            qg_out[0, :, h * LANES:(h + 1) * LANES] = (qh * gqa_scale).astype(BF16)
        kns = []
        for j in range(GQA_KV_HEADS):
            c0 = GQA_HEADS * LANES + j * LANES
            kn = _rms(g[:, c0:c0 + LANES]) * gk_ref[...]
            kns.append(kn)
            kr = _rope(kn, gcos, gsin, GQA_HEAD_DIM // 4) if has_pos else kn
            kg_ref[0, :, j * LANES:(j + 1) * LANES] = kr.astype(BF16)
        v0 = (GQA_HEADS + GQA_KV_HEADS) * LANES
        v = g[:, v0:v0 + GQA_KV_HEADS * LANES]
        for j in range(GQA_KV_HEADS):
            vj = jnp.where(first_half, v[:, j * LANES:(j + 1) * LANES], ones_lane)
            vg_ref[0, :, j * LANES:(j + 1) * LANES] = vj.astype(BF16)
        if emit_cache:
            ckv_out[0] = ckv
            kpe_out[0] = kpe[:, MLA_NOPE:MLA_NOPE + MLA_ROPE]
            kn_out[0] = jnp.where(first_half, kns[0], kns[1])
            v_out[0] = jnp.where(first_half, v[:, 0:LANES], v[:, LANES:2 * LANES])

    def ctx_tile():
        write_mla_kv(cckv_ref[0], ckr_ref[0])
        k = cgk_ref[0]
        k_sw = pltpu.roll(k, GQA_HEAD_DIM, 1)
        kg_ref[0, :, 0:LANES] = jnp.where(first_half, k, k_sw).astype(BF16)
        kg_ref[0, :, LANES:2 * LANES] = jnp.where(first_half, k_sw, k).astype(BF16)
        v = cgv_ref[0]
        vg_ref[0, :, 0:LANES] = jnp.where(first_half, v, ones_lane).astype(BF16)
        vg_ref[0, :, LANES:2 * LANES] = jnp.where(first_half, pltpu.roll(v, GQA_HEAD_DIM, 1), ones_lane).astype(BF16)

    if nct:
        pl.when(t < nct)(ctx_tile)
        pl.when(t >= nct)(new_tile)
    else:
        new_tile()


def _attention_operands(proj, wts, tables, ctx, layer, emit_cache, tm):
    b, l, _ = proj.shape
    has_pos = tables is not None
    nct = 0 if ctx is None else ctx[0].shape[2] // tm
    lk = l + nct * tm
    nt = lk // tm

    def new_t(t):
        return jnp.maximum(t - nct, 0) if nct else t

    in_specs = [
        pl.BlockSpec((1, tm, 512), lambda i, t: (i, new_t(t), COL_MLA // 512)),
        pl.BlockSpec((1, tm, 1024), lambda i, t: (i, new_t(t), COL_GQA // 1024)),
    ] + [_const_spec(w.shape) for w in wts]
    args = [proj, proj] + list(wts)
    if has_pos:
        in_specs += [pl.BlockSpec((tm, LANES), lambda i, t: (new_t(t), 0))] * 4
        args += list(tables)
    if nct:
        in_specs += [pl.BlockSpec((1, None, tm, LANES), lambda i, t: (i, layer, jnp.minimum(t, nct - 1), 0))] * 4
        args += list(ctx)
    q_spec = lambda w: pl.BlockSpec((1, tm, w), lambda i, t: (i, new_t(t), 0))
    k_spec = lambda w: pl.BlockSpec((1, tm, w), lambda i, t: (i, t, 0))
    out_specs = [q_spec(512), k_spec(512), k_spec(512), q_spec(512), k_spec(256), k_spec(256)]
    out_shape = [
        jax.ShapeDtypeStruct((b, l, 512), BF16), jax.ShapeDtypeStruct((b, lk, 512), BF16),
        jax.ShapeDtypeStruct((b, lk, 512), BF16), jax.ShapeDtypeStruct((b, l, 512), BF16),
        jax.ShapeDtypeStruct((b, lk, 256), BF16), jax.ShapeDtypeStruct((b, lk, 256), BF16),
    ]
    if emit_cache:
        out_specs += [q_spec(MLA_KV_LORA), q_spec(MLA_ROPE), q_spec(LANES), q_spec(LANES)]
        out_shape += [
            jax.ShapeDtypeStruct((b, l, MLA_KV_LORA), F32), jax.ShapeDtypeStruct((b, l, MLA_ROPE), F32),
            jax.ShapeDtypeStruct((b, l, LANES), F32), jax.ShapeDtypeStruct((b, l, LANES), F32),
        ]
    return pl.pallas_call(
        functools.partial(_qkv_kernel, nct=nct, has_pos=has_pos, emit_cache=emit_cache),
        grid=(b, nt),
        in_specs=in_specs,
        out_specs=out_specs,
        out_shape=out_shape,
        compiler_params=_cparams(2),
        name="attention_operands",
    )(*args)


def _attn_kernel(q_ref, k_ref, v_ref, o_ref, *, shared_kv):
    lane = lax.broadcasted_iota(jnp.int32, (1, LANES), 1)
    for i in range(q_ref.shape[0]):
        outs = []
        for a in range(2):
            cols = slice(0, LANES) if shared_kv else slice(a * LANES, (a + 1) * LANES)
            qa = q_ref[i, :, a * LANES:(a + 1) * LANES]
            s = lax.dot_general(qa, k_ref[i, :, cols], NT_DIMS, preferred_element_type=F32)
            p = jnp.exp2(s - jnp.max(s, axis=-1, keepdims=True)).astype(BF16)
            acc = _dot(p, v_ref[i, :, cols])
            outs.append(acc / acc[:, GQA_HEAD_DIM:GQA_HEAD_DIM + 1])
        o_ref[i] = jnp.where(lane < GQA_HEAD_DIM, outs[0], pltpu.roll(outs[1], GQA_HEAD_DIM, 1)).astype(BF16)


def _attention(q, k, v, shared_kv, tq, nb):
    b, l, _ = q.shape
    lk = k.shape[1]
    kw = LANES if shared_kv else 2 * LANES
    return pl.pallas_call(
        functools.partial(_attn_kernel, shared_kv=shared_kv),
        grid=(b // nb, 2, l // tq),
        in_specs=[
            pl.BlockSpec((nb, tq, 2 * LANES), lambda i, j, t: (i, t, j)),
            pl.BlockSpec((nb, lk, kw), lambda i, j, t: (i, 0, j)),
            pl.BlockSpec((nb, lk, kw), lambda i, j, t: (i, 0, j)),
        ],
        out_specs=pl.BlockSpec((nb, tq, LANES), lambda i, j, t: (i, t, j)),
        out_shape=jax.ShapeDtypeStruct((b, l, 2 * LANES), BF16),
        compiler_params=_cparams(3),
        name="attention_shared_kv" if shared_kv else "attention",
    )(q, k, v)


def _post_kernel(*refs, moe):
    it = iter(refs)
    ys_ref, z_ref, ym_ref, yg_ref, x_ref, g1_ref, sh2_ref, sc2_ref = (next(it) for _ in range(8))
    ng_ref, gpost_ref, gpre_ref, wout_ref = (next(it) for _ in range(4))
    wr_ref = next(it) if moe else None
    x1_ref, h2_ref = next(it), next(it)
    comb_ref, combt_ref = (next(it), next(it)) if moe else (None, None)

    y = _rms(ys_ref[0] * _silu(z_ref[0])) * ng_ref[...]
    cat = jnp.concatenate([y.astype(BF16), ym_ref[0], yg_ref[0]], axis=1)
    mix = _dot(cat, wout_ref[...])
    x1 = x_ref[0] + g1_ref[0] * (_rms(mix) * gpost_ref[...])
    x1_ref[0] = x1
    h2 = _rms(x1) * gpre_ref[...] * (1.0 + sc2_ref[0]) + sh2_ref[0]
    h16 = h2.astype(BF16)
    h2_ref[0] = h16
    if moe:
        w_hi = wr_ref[...].astype(BF16)
        w_lo = (wr_ref[...] - w_hi.astype(F32)).astype(BF16)
        logits = _dot(h16, w_hi) + _dot(h16, w_lo)
        lane = lax.broadcasted_iota(jnp.int32, (1, LANES), 1).astype(F32)
        lg = jnp.where(lane < N_EXPERTS, logits, -jnp.inf)
        m1 = jnp.max(lg, axis=-1, keepdims=True)
        i1 = jnp.min(jnp.where(lg == m1, lane, float(LANES)), axis=-1, keepdims=True)
        lg2 = jnp.where(lane == i1, -jnp.inf, lg)
        m2 = jnp.max(lg2, axis=-1, keepdims=True)
        i2 = jnp.min(jnp.where(lg2 == m2, lane, float(LANES)), axis=-1, keepdims=True)
        e = jnp.exp(m2 - m1)
        comb = jnp.where(lane == i1, 1.0 / (1.0 + e), jnp.where(lane == i2, e / (1.0 + e), 0.0))
        comb_ref[0] = comb
        combt_ref[...] = comb.T[0:N_EXPERTS, :]


def _post_attention(yssd, proj, ym, yg, x, mod, mod_row, ng, gpost, gpre, w_out, w_router, tm):
    b, l, d = x.shape
    moe = w_router is not None
    tok = lambda w: pl.BlockSpec((1, tm, w), lambda i, t: (i, t, 0))
    modk = lambda k: pl.BlockSpec((1, 1, d), lambda i, t: (mod_row(i), 0, k))
    in_specs = [tok(SSD_INNER), tok(SSD_INNER), tok(256), tok(256), tok(d), modk(2), modk(3), modk(4),
                _const_spec(ng.shape), _const_spec(gpost.shape), _const_spec(gpre.shape),
                _const_spec(w_out.shape)]
    args = [yssd, proj, ym, yg, x, mod, mod, mod, ng, gpost, gpre, w_out]
    out_specs = [tok(d), tok(d)]
    out_shape = [jax.ShapeDtypeStruct((b, l, d), F32), jax.ShapeDtypeStruct((b, l, d), BF16)]
    if moe:
        in_specs.append(_const_spec(w_router.shape))
        args.append(w_router)
        out_specs += [tok(LANES), pl.BlockSpec((N_EXPERTS, tm), lambda i, t: (0, i * (l // tm) + t))]
        out_shape += [jax.ShapeDtypeStruct((b, l, LANES), F32), jax.ShapeDtypeStruct((N_EXPERTS, b * l), F32)]
    outs = pl.pallas_call(
        functools.partial(_post_kernel, moe=moe),
        grid=(b, l // tm),
        in_specs=in_specs,
        out_specs=out_specs,
        out_shape=out_shape,
        compiler_params=_cparams(2),
        name="post_attention",
    )(*args)
    return outs if moe else (outs[0], outs[1], None, None)


def _ffn_kernel(h_ref, x1_ref, g2_ref, gp_ref, wg_ref, wu_ref, wd_ref, o_ref):
    h = h_ref[0]
    act = (_silu(_dot(h, wg_ref[...])) * _dot(h, wu_ref[...])).astype(BF16)
    f = _dot(act, wd_ref[...])
    o_ref[0] = x1_ref[0] + g2_ref[0] * (_rms(f) * gp_ref[...])


def _dense_ffn(h2, x1, mod, mod_row, gp, wg, wu, wd, tm):
    b, l, d = x1.shape
    tok = pl.BlockSpec((1, tm, d), lambda i, t: (i, t, 0))
    return pl.pallas_call(
        _ffn_kernel,
        grid=(b, l // tm),
        in_specs=[tok, tok, pl.BlockSpec((1, 1, d), lambda i, t: (mod_row(i), 0, 5)),
                  _const_spec(gp.shape), _const_spec(wg.shape), _const_spec(wu.shape), _const_spec(wd.shape)],
        out_specs=tok,
        out_shape=jax.ShapeDtypeStruct((b, l, d), F32),
        compiler_params=_cparams(2),
        name="dense_ffn",
    )(h2, x1, mod, gp, wg, wu, wd)


MOE_BLOCK = 1024
MOE_CAP = 288


def _moe_kernel(h_ref, comb_ref, combt_ref, x1_ref, g2_ref, gp_ref, wg_ref, wu_ref, wd_ref, o_ref,
                pos_s, *, cap):
    e = pl.program_id(1)
    tb = o_ref.shape[0]

    @pl.when(e == 0)
    def _():
        o_ref[...] = jnp.zeros(o_ref.shape, F32)
        si = lax.broadcasted_iota(jnp.int32, (tb, tb), 0)
        ti = lax.broadcasted_iota(jnp.int32, (tb, tb), 1)
        before = jnp.where(si < ti, 1.0, 0.0).astype(BF16)
        sel = jnp.where(combt_ref[...] > 0.0, 1.0, 0.0).astype(BF16)
        pos_s[...] = _dot(sel, before)

    sel_e = combt_ref[pl.ds(e, 1), :] > 0.0
    pos_e = pos_s[pl.ds(e, 1), :]
    n_e = jnp.sum(jnp.where(sel_e, 1.0, 0.0)).astype(jnp.int32)
    lane = lax.broadcasted_iota(jnp.int32, (1, LANES), 1)
    gate = jnp.sum(jnp.where(lane == e, comb_ref[...], 0.0), axis=-1, keepdims=True)
    slot = lax.broadcasted_iota(jnp.int32, (cap, 1), 0).astype(F32)

    def tile(j, carry):
        first = j * float(cap)
        pick = jnp.where(sel_e & (pos_e - first == slot), 1.0, 0.0).astype(BF16)
        xg = _dot(pick, h_ref[...]).astype(BF16)
        act = (_silu(_dot(xg, wg_ref[0])) * _dot(xg, wu_ref[0])).astype(BF16)
        y = _dot(act, wd_ref[0]).astype(BF16)
        o_ref[...] += gate * lax.dot_general(pick, y, TN_DIMS, preferred_element_type=F32)
        return carry

    lax.fori_loop(0, (n_e + cap - 1) // cap, tile, 0)

    @pl.when(e == pl.num_programs(1) - 1)
    def _():
        o_ref[...] = x1_ref[...] + g2_ref[0] * (_rms(o_ref[...]) * gp_ref[...])


def _moe_ffn(h2, comb, combt, x1, mod, mod_row, gp, wg, wu, wd, tb):
    t, d = x1.shape
    n_e, _, ff = wg.shape
    cap = -(-(MOE_CAP * tb // MOE_BLOCK) // 16) * 16
    return pl.pallas_call(
        functools.partial(_moe_kernel, cap=cap),
        grid=(t // tb, n_e),
        in_specs=[pl.BlockSpec((tb, d), lambda s, e: (s, 0)),
                  pl.BlockSpec((tb, LANES), lambda s, e: (s, 0)),
                  pl.BlockSpec((N_EXPERTS, tb), lambda s, e: (0, s)),
                  pl.BlockSpec((tb, d), lambda s, e: (s, 0), pipeline_mode=pl.Buffered(1)),
                  pl.BlockSpec((1, 1, d), lambda s, e: (mod_row(s), 0, 5)),
                  pl.BlockSpec((1, d), lambda s, e: (0, 0)),
                  pl.BlockSpec((1, d, ff), lambda s, e: (e, 0, 0)),
                  pl.BlockSpec((1, d, ff), lambda s, e: (e, 0, 0)),
                  pl.BlockSpec((1, ff, d), lambda s, e: (e, 0, 0))],
        out_specs=pl.BlockSpec((tb, d), lambda s, e: (s, 0)),
        out_shape=jax.ShapeDtypeStruct((t, d), F32),
        scratch_shapes=[pltpu.VMEM((N_EXPERTS, tb), F32)],
        compiler_params=_cparams(2),
        name="moe_ffn",
    )(h2, comb, combt, x1, mod, gp, wg, wu, wd)


def _in_proj_columns():
    o_dt = 1536
    o_ql, o_kv, o_kpe = 1552, 1808, 1936
    o_gq, o_gk, o_gv = 1968, 2224, 2352
    idx = np.full((IN_COLS_PADDED,), -1, np.int64)
    idx[0:1536] = np.arange(1536)
    idx[COL_MLA:COL_MLA + 256] = o_ql + np.arange(256)
    idx[COL_MLA + 256:COL_MLA + 384] = o_kv + np.arange(128)
    idx[COL_MLA + 384 + MLA_NOPE:COL_MLA + 384 + MLA_NOPE + MLA_ROPE] = o_kpe + np.arange(MLA_ROPE)
    for h in range(GQA_HEADS):
        idx[COL_GQA + h * LANES:COL_GQA + h * LANES + 64] = o_gq + h * 64 + np.arange(64)
    for j in range(GQA_KV_HEADS):
        for r in range(2):
            c0 = COL_GQA + 512 + j * LANES + r * 64
            idx[c0:c0 + 64] = o_gk + j * 64 + np.arange(64)
            idx[c0 + 256:c0 + 256 + 64] = o_gv + j * 64 + np.arange(64)
    for g in range(SSD_GROUPS):
        for d in range(2):
            c0 = COL_DT + g * LANES + d * HEADS_PER_GROUP
            idx[c0:c0 + HEADS_PER_GROUP] = o_dt + d * SSD_HEADS + g * HEADS_PER_GROUP + np.arange(HEADS_PER_GROUP)
    return idx


def _gather_cols(w, idx):
    cols = jnp.take(w, jnp.asarray(np.maximum(idx, 0)), axis=1)
    return jnp.where(jnp.asarray(idx >= 0)[None, :], cols, 0.0)


def _mla_weight_columns():
    per_q = MLA_NOPE + MLA_ROPE
    uq = np.full((MLA_HEADS * LANES,), -1, np.int64)
    uk = np.full((MLA_HEADS * LANES,), -1, np.int64)
    uv = np.full((MLA_HEADS * LANES,), -1, np.int64)
    for h in range(MLA_HEADS):
        uq[h * LANES:h * LANES + per_q] = h * per_q + np.arange(per_q)
        uk[h * LANES:h * LANES + MLA_NOPE] = h * (MLA_NOPE + MLA_V) + np.arange(MLA_NOPE)
        uv[h * LANES:h * LANES + MLA_V] = h * (MLA_NOPE + MLA_V) + MLA_NOPE + np.arange(MLA_V)
    return uq, uk, uv


def _rope_tables(n_tokens):
    t = np.arange(n_tokens)
    row, colp = (t // GRID_W).astype(np.float64), (t % GRID_W).astype(np.float64)

    def table(width, lane0, head_dim, reps):
        cos = np.ones((n_tokens, width))
        sin = np.zeros((n_tokens, width))
        half = head_dim // 2
        quarter = half // 2
        for dim in range(head_dim):
            pos = row if dim < half else colp
            inv = ROPE_BASE ** (-(2.0 * (dim % quarter)) / half)
            ang = pos * inv
            sign = -1.0 if (dim % half) < quarter else 1.0
            for r in range(reps):
                cos[:, lane0 + r * head_dim + dim] = np.cos(ang)
                sin[:, lane0 + r * head_dim + dim] = sign * np.sin(ang)
        return jnp.asarray(cos, F32), jnp.asarray(sin, F32)

    mcos, msin = table(LANES, MLA_NOPE, MLA_ROPE, 1)
    gcos, gsin = table(LANES, 0, GQA_HEAD_DIM, 2)
    return mcos, msin, gcos, gsin


def _trunk_layer(i, x, mod_i, mod_row, p, tables, ctx):
    b, l, d = x.shape
    tm = min(512, l)
    emit = ctx is None
    proj = _in_projection(x, mod_i, mod_row, p["g_mix_pre"][i], p["w_in"][i], tm)
    h0 = None if ctx is None else ctx["state"][:, i]
    yssd, state = _ssd_mixer(proj, p["conv_w"][i], p["conv_b"][i], p["ssd_par"][i], h0, emit)
    ctx_kv = None if ctx is None else ctx["kv"]
    ops = _attention_operands(proj, p["attn_w"][i], tables, ctx_kv, i, emit, tm)
    qm, km, vm, qg, kg, vg = ops[:6]
    tq = min(256, l)
    nb = math.gcd(b, max(1, 1024 // km.shape[1]))
    ym = _attention(qm, km, vm, False, tq, nb)
    yg = _attention(qg, kg, vg, True, tq, nb)
    moe = i % 2 == 1
    j = i // 2
    x1, h2, comb, combt = _post_attention(
        yssd, proj, ym, yg, x, mod_i, mod_row, p["ssd_norm_g"][i], p["g_mix_post"][i], p["g_ffn_pre"][i],
        p["w_out"][i], p["w_router"][j] if moe else None, tm)
    if moe:
        per_batch = ctx is not None
        tb = min(MOE_BLOCK, l if per_batch else b * l)
        blk_row = (lambda s: mod_row(s // (l // tb))) if per_batch else mod_row
        x2 = _moe_ffn(h2.reshape(b * l, d), comb.reshape(b * l, LANES), combt, x1.reshape(b * l, d), mod_i,
                      blk_row, p["g_ffn_post"][i], p["moe_wg"][j], p["moe_wu"][j], p["moe_wd"][j], tb)
        x2 = x2.reshape(b, l, d)
    else:
        x2 = _dense_ffn(h2, x1, mod_i, mod_row, p["g_ffn_post"][i],
                        p["ffn_wg"][j], p["ffn_wu"][j], p["ffn_wd"][j], tm)
    return x2, (state,) + tuple(ops[6:])


def kernel(x_prompt, x_sample, state_ssd, cache_mla_ckv, cache_mla_krope, cache_gqa_k, cache_gqa_v, c, c_ctx, w_mod, b_mod, g_mix_pre, g_mix_post, g_ffn_pre, g_ffn_post, w_in, ssd_conv_w, ssd_conv_b, ssd_A_log, ssd_dt_bias, ssd_D, ssd_norm_g, mla_q_norm_g, mla_w_uq, mla_kv_norm_g, mla_w_ukv, gqa_q_norm_g, gqa_k_norm_g, w_out, ffn_w_gate, ffn_w_up, ffn_w_down, moe_w_router, moe_w_gate, moe_w_up, moe_w_down):
    depth = w_in.shape[0]
    d = x_prompt.shape[-1]
    n_dec = x_sample.shape[0]
    hpg = HEADS_PER_GROUP

    in_idx = _in_proj_columns()
    uq_idx, uk_idx, uv_idx = _mla_weight_columns()
    row2 = lambda a: a.reshape(depth, 1, a.shape[-1])
    def ssd_rows(fwd, bwd):
        cols = [fwd.reshape(depth, SSD_GROUPS, hpg), bwd.reshape(depth, SSD_GROUPS, hpg),
                jnp.zeros((depth, SSD_GROUPS, SSD_ROWS - 2 * hpg), F32)]
        return jnp.concatenate(cols, axis=-1)

    par = jnp.concatenate([ssd_rows(ssd_dt_bias[:, 0], ssd_dt_bias[:, 1]),
                           ssd_rows(ssd_A_log[:, 0], ssd_A_log[:, 1]),
                           ssd_rows(ssd_D, jnp.zeros_like(ssd_D))], axis=-1)
    par = jnp.broadcast_to(par[..., None], par.shape + (LANES,))
    zeros64 = jnp.zeros((depth, 1, GQA_HEAD_DIM), F32)
    attn_w = []
    for i in range(depth):
        attn_w.append((
            mla_q_norm_g[i][None, :], mla_kv_norm_g[i][None, :],
            _gather_cols(mla_w_uq[i], uq_idx).astype(BF16),
            _gather_cols(mla_w_ukv[i], uk_idx).astype(BF16),
            _gather_cols(mla_w_ukv[i], uv_idx).astype(BF16),
            jnp.concatenate([gqa_q_norm_g[i][None, :], zeros64[i]], axis=1),
            jnp.concatenate([gqa_k_norm_g[i][None, :]] * 2, axis=1),
        ))
    p = dict(
        g_mix_pre=row2(g_mix_pre), g_mix_post=row2(g_mix_post), g_ffn_pre=row2(g_ffn_pre),
        g_ffn_post=row2(g_ffn_post), ssd_norm_g=row2(ssd_norm_g),
        w_in=[_gather_cols(w_in[i], in_idx).astype(BF16) for i in range(depth)],
        conv_w=ssd_conv_w, conv_b=row2(ssd_conv_b), ssd_par=par, attn_w=attn_w,
        w_out=w_out.astype(BF16),
        ffn_wg=ffn_w_gate.astype(BF16), ffn_wu=ffn_w_up.astype(BF16), ffn_wd=ffn_w_down.astype(BF16),
        w_router=jnp.pad(moe_w_router, ((0, 0), (0, 0), (0, LANES - N_EXPERTS))),
        moe_wg=moe_w_gate.astype(BF16), moe_wu=moe_w_up.astype(BF16), moe_wd=moe_w_down.astype(BF16),
    )

    rows = -(-(1 + n_dec) // 8) * 8
    c_all = jnp.concatenate([c_ctx[None, :], c, jnp.zeros((rows - 1 - n_dec, d), F32)], axis=0)
    mod = _modulation(c_all, w_mod, b_mod).reshape(depth, rows, 1, 6 * d)

    y = x_prompt
    collected = [[], [], [], [], []]
    for i in range(depth):
        y, outs = _trunk_layer(i, y, mod[i], lambda bi: 0, p, None, None)
        for lst, t in zip(collected, outs):
            lst.append(t)
    y_prompt = y
    bsz, seq = x_prompt.shape[:2]
    new_state = jnp.stack(collected[0], axis=1)
    new_ckv = jnp.stack(collected[1], axis=1)
    new_krope = jnp.stack(collected[2], axis=1)
    new_k = jnp.stack(collected[3], axis=1).reshape(bsz, depth, seq, GQA_KV_HEADS, GQA_HEAD_DIM)
    new_v = jnp.stack(collected[4], axis=1).reshape(bsz, depth, seq, GQA_KV_HEADS, GQA_HEAD_DIM)

    past = cache_mla_ckv.shape[2]
    krope_blk = jnp.pad(cache_mla_krope, ((0, 0), (0, 0), (0, 0), (MLA_NOPE, LANES - MLA_NOPE - MLA_ROPE)))
    ctx = dict(
        state=state_ssd,
        kv=(cache_mla_ckv, krope_blk,
            cache_gqa_k.reshape(n_dec, depth, past, LANES), cache_gqa_v.reshape(n_dec, depth, past, LANES)),
    )
    tables = _rope_tables(x_sample.shape[1])
    y = x_sample
    for i in range(depth):
        y, _ = _trunk_layer(i, y, mod[i], lambda bi: bi + 1, p, tables, ctx)
    return (y_prompt, y, new_state, new_ckv, new_krope, new_k, new_v)
```

```python
import functools
import itertools
import math

import numpy as np
import jax
import jax.numpy as jnp
from jax import lax
from jax.experimental import pallas as pl
from jax.experimental.pallas import tpu as pltpu

F32 = jnp.float32
BF16 = jnp.bfloat16

EPS = 1e-6
ROPE_BASE = 10000.0
GRID_W = 64
SSD_HEAD_DIM = 64
SSD_HEADS = 8
SSD_GROUPS = 2
SSD_STATE = 128
SSD_CHUNK = 128
SSD_INNER = SSD_HEADS * SSD_HEAD_DIM
HEADS_PER_GROUP = SSD_HEADS // SSD_GROUPS
SSD_ROWS = 16
MLA_HEADS = 4
MLA_V = 64
MLA_NOPE = 64
MLA_ROPE = 32
MLA_Q_LORA = 256
MLA_KV_LORA = 128
GQA_HEADS = 4
GQA_KV_HEADS = 2
GQA_HEAD_DIM = 64
N_EXPERTS = 8

LANES = 128
BF16_SUBLANES = 16

COL_Z = 0
COL_X = 512
COL_B = 1024
COL_C = 1280
COL_MLA = 1536
COL_GQA = 2048
COL_DT = 3072
IN_COLS_PADDED = 3328

VMEM_LIMIT = 56 * 1024 * 1024

NT_DIMS = (((1,), (1,)), ((), ()))
TN_DIMS = (((0,), (0,)), ((), ()))


def _cparams(n_grid):
    return pltpu.CompilerParams(
        dimension_semantics=("arbitrary",) * n_grid, vmem_limit_bytes=VMEM_LIMIT)


def _const_spec(shape):
    nd = len(shape)
    return pl.BlockSpec(shape, lambda *_: (0,) * nd, pipeline_mode=pl.Buffered(1))


def _dot(a, b):
    return jnp.dot(a, b, preferred_element_type=F32)


def _rms(x, width=None):
    n = x.shape[-1] if width is None else width
    return x * lax.rsqrt(jnp.sum(x * x, axis=-1, keepdims=True) * (1.0 / n) + EPS)


def _silu(x):
    return x * jax.nn.sigmoid(x)


def _split3(v):
    hi = v.astype(BF16)
    r = v - hi.astype(F32)
    mid = r.astype(BF16)
    lo = (r - mid.astype(F32)).astype(BF16)
    return hi, mid, lo


def _mod_kernel(c_ref, w_ref, b_ref, o_ref):
    s = _silu(c_ref[...]).astype(BF16)
    o_ref[0] = _dot(s, w_ref[0].astype(BF16)) + b_ref[0]


def _modulation(c_all, w_mod, b_mod):
    depth, d, n = w_mod.shape
    tn = 1536
    rows = c_all.shape[0]
    return pl.pallas_call(
        _mod_kernel,
        grid=(depth, n // tn),
        in_specs=[
            pl.BlockSpec((rows, d), lambda i, j: (0, 0)),
            pl.BlockSpec((1, d, tn), lambda i, j: (i, 0, j)),
            pl.BlockSpec((1, 1, tn), lambda i, j: (i, 0, j)),
        ],
        out_specs=pl.BlockSpec((1, rows, tn), lambda i, j: (i, 0, j)),
        out_shape=jax.ShapeDtypeStruct((depth, rows, n), F32),
        compiler_params=_cparams(2),
        name="modulation",
    )(c_all, w_mod, b_mod.reshape(depth, 1, n))


def _inproj_kernel(x_ref, sh_ref, sc_ref, g_ref, w_ref, o_ref, dt_ref):
    h = _rms(x_ref[0]) * g_ref[...]
    h = h * (1.0 + sc_ref[0]) + sh_ref[0]
    proj = _dot(h.astype(BF16), w_ref[...])
    o_ref[0] = proj[:, 0:COL_DT].astype(BF16)
    dt_ref[0] = proj[:, COL_DT:]


def _in_projection(x, mod, mod_row, gain, w_in_p, tm):
    b, l, d = x.shape
    n = w_in_p.shape[1]
    n_dt = n - COL_DT
    return pl.pallas_call(
        _inproj_kernel,
        grid=(b, l // tm),
        in_specs=[
            pl.BlockSpec((1, tm, d), lambda i, t: (i, t, 0)),
            pl.BlockSpec((1, 1, d), lambda i, t: (mod_row(i), 0, 0)),
            pl.BlockSpec((1, 1, d), lambda i, t: (mod_row(i), 0, 1)),
            _const_spec((1, d)),
            _const_spec((d, n)),
        ],
        out_specs=[pl.BlockSpec((1, tm, COL_DT), lambda i, t: (i, t, 0)),
                   pl.BlockSpec((1, tm, n_dt), lambda i, t: (i, t, 0))],
        out_shape=[jax.ShapeDtypeStruct((b, l, COL_DT), BF16), jax.ShapeDtypeStruct((b, l, n_dt), F32)],
        compiler_params=_cparams(2),
        name="in_projection",
    )(x, mod, mod, gain, w_in_p)


def _ssd_kernel(*refs, nc, seq, has_h0, emit_state):
    it = iter(refs)
    x_ref, b_ref, c_ref, dt_ref = next(it), next(it), next(it), next(it)
    cwx_ref, cwb_ref, cwc_ref = next(it), next(it), next(it)
    cbx_ref, cbb_ref, cbc_ref = next(it), next(it), next(it)
    par_ref = next(it)
    h0_ref = next(it) if has_h0 else None
    y_ref = next(it)
    st_ref = next(it) if emit_state else None
    xt_s, yd_s, bc_s, dtv_s, arow_s, acol_s, h_s = (next(it) for _ in range(7))

    q = SSD_CHUNK
    hpg = HEADS_PER_GROUP
    hd = SSD_HEAD_DIM
    wx = hpg * hd
    r = SSD_ROWS
    pack = BF16_SUBLANES
    row = lax.broadcasted_iota(jnp.int32, (q, 1), 0)
    si = lax.broadcasted_iota(jnp.int32, (q, q), 0)
    ti = lax.broadcasted_iota(jnp.int32, (q, q), 1)
    le = si <= ti
    ge = si >= ti
    tri_le = jnp.where(le, 1.0, 0.0).astype(BF16)
    tri_ge = jnp.where(ge, 1.0, 0.0).astype(BF16)
    par = par_ref[0]
    bias_t, alog_t, d_t = par[0:r], par[r:2 * r], par[2 * r:3 * r]

    for d in range(2):
        if has_h0:
            h_s[d] = h0_ref[0, d].reshape(wx, SSD_STATE)
        else:
            h_s[d] = jnp.zeros((wx, SSD_STATE), F32)

    def conv_silu(ref, w_ref, bias_ref, c):
        off = pl.multiple_of(c * q, q)
        u = ref[0, pl.ds(off, q), :].astype(F32)
        before = ref[0, pl.ds(pl.multiple_of(jnp.maximum(off - pack, 0), pack), pack), :]
        after = ref[0, pl.ds(pl.multiple_of(jnp.minimum(off + q, seq - pack), pack), pack), :]
        prev = before[pack - 1:pack, :].astype(F32) * jnp.where(c > 0, 1.0, 0.0)
        nxt = after[0:1, :].astype(F32) * jnp.where(c < nc - 1, 1.0, 0.0)
        up = jnp.where(row == 0, prev, pltpu.roll(u, 1, 0))
        un = jnp.where(row == q - 1, nxt, pltpu.roll(u, q - 1, 0))
        w = w_ref[...]
        return _silu(up * w[0:1] + u * w[1:2] + un * w[2:3] + bias_ref[...])

    tri_rows = jnp.concatenate([tri_le, tri_ge], axis=1)
    tri_cols = jnp.concatenate([tri_ge, tri_le], axis=0)
    fwd_row = lax.broadcasted_iota(jnp.int32, (r, 1), 0) < hpg
    fwd_lane = lax.broadcasted_iota(jnp.int32, (1, r), 1) < hpg

    def prep_body(c, carry):
        off = pl.multiple_of(c * q, q)
        xt = conv_silu(x_ref, cwx_ref, cbx_ref, c).T
        xt_s[c] = xt
        bc_s[pl.ds(off, q), 0:SSD_STATE] = conv_silu(b_ref, cwb_ref, cbb_ref, c).astype(BF16)
        bc_s[pl.ds(off, q), SSD_STATE:2 * SSD_STATE] = conv_silu(c_ref, cwc_ref, cbc_ref, c).astype(BF16)
        dtr = dt_ref[0, pl.ds(off, q), :].T[0:r, :] + bias_t
        dtv = jnp.maximum(dtr, 0.0) + jnp.log1p(jnp.exp(-jnp.abs(dtr)))
        dtv_s[c] = dtv
        parts = _split3(dtv * (-jnp.exp(alog_t)))
        rows = sum(_dot(part, tri_rows) for part in parts)
        cols = sum(lax.dot_general(tri_cols, part, NT_DIMS, preferred_element_type=F32) for part in parts)
        arow_s[c] = jnp.where(fwd_row, rows[:, 0:q], rows[:, q:2 * q])
        acol_s[c] = jnp.where(fwd_lane, cols[0:q, :], cols[q:2 * q, :])
        return carry

    def chunk(c, d):
        off = pl.multiple_of(c * q, q)
        base, mask = (0, le) if d == 0 else (hpg, ge)
        xt = xt_s[c]
        b16 = bc_s[pl.ds(off, q), 0:SSD_STATE]
        c16 = bc_s[pl.ds(off, q), SSD_STATE:2 * SSD_STATE]
        dtv, a_rows, a_cols = dtv_s[c], arow_s[c], acol_s[c]
        tot = a_rows[:, q - 1:q] if d == 0 else a_rows[:, 0:1]
        st = lax.dot_general(b16, c16, NT_DIMS, preferred_element_type=F32)
        hs = h_s[d]
        yoff = lax.dot_general(hs.astype(BF16), c16, NT_DIMS, preferred_element_type=F32)
        yield
        outs, xdecs, hnew = [], [], []
        for hl in range(hpg):
            k = base + hl
            rs = slice(hl * hd, (hl + 1) * hd)
            a_t = a_rows[k:k + 1, :]
            tot_k = tot[k:k + 1, :]
            seg = a_t - a_cols[:, k:k + 1]
            w = (st * jnp.exp(jnp.where(mask, seg, -jnp.inf))).astype(BF16)
            xdt = xt[rs, :] * dtv[k:k + 1, :]
            outs.append(_dot(xdt.astype(BF16), w) + yoff[rs, :] * jnp.exp(a_t))
            xdecs.append((xdt * jnp.exp(tot_k - a_t)).astype(BF16))
            hnew.append(hs[rs, :] * jnp.exp(tot_k))
            yield
        h_s[d] = jnp.concatenate(hnew, axis=0) + _dot(jnp.concatenate(xdecs, axis=0), b16)
        yd_s[d, c] = jnp.concatenate(outs, axis=0)

    def scan_body(i, carry):
        for _ in itertools.zip_longest(chunk(i, 0), chunk(nc - 1 - i, 1)):
            pass
        return carry

    def out_body(c, carry):
        xt = xt_s[c]
        skip = [xt[hl * hd:(hl + 1) * hd, :] * d_t[hl:hl + 1, :] for hl in range(hpg)]
        yt = yd_s[0, c] + yd_s[1, c] + jnp.concatenate(skip, axis=0)
        y_ref[0, pl.ds(pl.multiple_of(c * q, q), q), :] = yt.T
        return carry

    lax.fori_loop(0, nc, prep_body, 0, unroll=2)
    lax.fori_loop(0, nc, scan_body, 0, unroll=2)
    lax.fori_loop(0, nc, out_body, 0, unroll=4)
    if emit_state:
        for d in range(2):
            st_ref[0, d] = h_s[d].reshape(hpg, SSD_HEAD_DIM, SSD_STATE)


def _ssd_mixer(proj, dt, conv_w, conv_b, par, h0, emit_state):
    b, l, _ = proj.shape
    g = SSD_GROUPS
    wx = HEADS_PER_GROUP * SSD_HEAD_DIM
    n = SSD_STATE
    nc = l // SSD_CHUNK
    has_h0 = h0 is not None

    def col(width, start):
        blk = start // width
        return pl.BlockSpec((1, l, width), lambda i, j: (i, 0, blk + j))

    def cw(width, start, rows):
        blk = start // width
        return pl.BlockSpec((rows, width), lambda i, j: (0, blk + j))

    in_specs = [
        col(wx, COL_X), col(n, COL_B), col(n, COL_C), col(LANES, 0),
        cw(wx, 0, 3), cw(n, SSD_INNER, 3), cw(n, SSD_INNER + g * n, 3),
        cw(wx, 0, 1), cw(n, SSD_INNER, 1), cw(n, SSD_INNER + g * n, 1),
        pl.BlockSpec((1, 3 * SSD_ROWS, LANES), lambda i, j: (j, 0, 0)),
    ]
    args = [proj, proj, proj, dt, conv_w, conv_w, conv_w, conv_b, conv_b, conv_b, par]
    state_spec = pl.BlockSpec((1, 2, HEADS_PER_GROUP, SSD_HEAD_DIM, n), lambda i, j: (i, 0, j, 0, 0))
    if has_h0:
        in_specs.append(state_spec)
        args.append(h0)
    out_specs = [pl.BlockSpec((1, l, wx), lambda i, j: (i, 0, j))]
    out_shape = [jax.ShapeDtypeStruct((b, l, SSD_INNER), F32)]
    if emit_state:
        out_specs.append(state_spec)
        out_shape.append(jax.ShapeDtypeStruct((b, 2, SSD_HEADS, SSD_HEAD_DIM, n), F32))
    outs = pl.pallas_call(
        functools.partial(_ssd_kernel, nc=nc, seq=l, has_h0=has_h0, emit_state=emit_state),
        grid=(b, g),
        in_specs=in_specs,
        out_specs=out_specs,
        out_shape=out_shape,
        scratch_shapes=[
            pltpu.VMEM((nc, wx, SSD_CHUNK), F32),
            pltpu.VMEM((2, nc, wx, SSD_CHUNK), F32),
            pltpu.VMEM((l, 2 * n), BF16),
            pltpu.VMEM((nc, SSD_ROWS, SSD_CHUNK), F32),
            pltpu.VMEM((nc, SSD_ROWS, SSD_CHUNK), F32),
            pltpu.VMEM((nc, SSD_CHUNK, SSD_ROWS), F32),
            pltpu.VMEM((2, wx, n), F32),
        ],
        compiler_params=_cparams(2),
        name="ssd_mixer",
    )(*args)
    return (outs[0], outs[1]) if emit_state else (outs[0], None)


def _rope(x, cos, sin_signed, pair):
    n = x.shape[1]
    lane = lax.broadcasted_iota(jnp.int32, (1, n), 1)
    first = (lane & (2 * pair - 1)) < pair
    partner = jnp.where(first, pltpu.roll(x, n - pair, 1), pltpu.roll(x, pair, 1))
    return x * cos + partner * sin_signed


def _tile_lanes(x, k):
    return jnp.concatenate([x] * k, axis=1)


def _qkv_kernel(*refs, nct, has_pos, emit_cache):
    it = iter(refs)
    mla_ref, gqa_ref = next(it), next(it)
    qg_ref, kvg_ref, wuq_ref, wk_ref, wv_ref, gq_ref, gk_ref = (next(it) for _ in range(7))
    if has_pos:
        mcos_ref, msin_ref, gcos_ref, gsin_ref = (next(it) for _ in range(4))
    if nct:
        cckv_ref, ckr_ref, cgk_ref, cgv_ref = (next(it) for _ in range(4))
    qm_ref, km_ref, vm_ref, qg_out, kg_ref, vg_ref = (next(it) for _ in range(6))
    if emit_cache:
        ckv_out, kpe_out, kn_out, v_out = (next(it) for _ in range(4))

    t = pl.program_id(1)
    lane = lax.broadcasted_iota(jnp.int32, (1, LANES), 1)
    first_half = lane < GQA_HEAD_DIM
    mla_scale = math.log2(math.e) / math.sqrt(MLA_NOPE + MLA_ROPE)
    gqa_scale = math.log2(math.e) / math.sqrt(GQA_HEAD_DIM)
    ones_lane = jnp.where(lane == GQA_HEAD_DIM, 1.0, 0.0)

    def write_mla_kv(ckv, kpe_r):
        c16 = ckv.astype(BF16)
        km_ref[0] = (_dot(c16, wk_ref[...]) + _tile_lanes(kpe_r, MLA_HEADS)).astype(BF16)
        vm_ref[0] = (_dot(c16, wv_ref[...]) + _tile_lanes(ones_lane, MLA_HEADS)).astype(BF16)

    def new_tile():
        m = mla_ref[0].astype(F32)
        q_lat = m[:, 0:MLA_Q_LORA]
        kv_lat = m[:, MLA_Q_LORA:MLA_Q_LORA + MLA_KV_LORA]
        kpe = m[:, MLA_Q_LORA + MLA_KV_LORA:]
        qn = _rms(q_lat) * qg_ref[...]
        qm = _dot(qn.astype(BF16), wuq_ref[...])
        ckv = _rms(kv_lat) * kvg_ref[...]
        kpe_r = kpe
        if has_pos:
            mcos, msin = mcos_ref[...], msin_ref[...]
            qm = _rope(qm, _tile_lanes(mcos, MLA_HEADS), _tile_lanes(msin, MLA_HEADS), MLA_ROPE // 4)
            kpe_r = _rope(kpe, mcos, msin, MLA_ROPE // 4)
        qm_ref[0] = (qm * mla_scale).astype(BF16)
        write_mla_kv(ckv, kpe_r)

        g = gqa_ref[0].astype(F32)
        if has_pos:
            gcos, gsin = gcos_ref[...], gsin_ref[...]
        for h in range(GQA_HEADS):
            xh = g[:, h * LANES:(h + 1) * LANES]
            qh = _rms(xh, GQA_HEAD_DIM) * gq_ref[...]
            if has_pos:
                qh = _rope(qh, gcos, gsin, GQA_HEAD_DIM // 4)
            qg_out[0, :, h * LANES:(h + 1) * LANES] = (qh * gqa_scale).astype(BF16)
        kns = []
        for j in range(GQA_KV_HEADS):
            c0 = GQA_HEADS * LANES + j * LANES
            kn = _rms(g[:, c0:c0 + LANES]) * gk_ref[...]
            kns.append(kn)
            kr = _rope(kn, gcos, gsin, GQA_HEAD_DIM // 4) if has_pos else kn
            kg_ref[0, :, j * LANES:(j + 1) * LANES] = kr.astype(BF16)
        v0 = (GQA_HEADS + GQA_KV_HEADS) * LANES
        v = g[:, v0:v0 + GQA_KV_HEADS * LANES]
        for j in range(GQA_KV_HEADS):
            vj = jnp.where(first_half, v[:, j * LANES:(j + 1) * LANES], ones_lane)
            vg_ref[0, :, j * LANES:(j + 1) * LANES] = vj.astype(BF16)
        if emit_cache:
            ckv_out[0] = ckv
            kpe_out[0] = kpe[:, MLA_NOPE:MLA_NOPE + MLA_ROPE]
            kn_out[0] = jnp.where(first_half, kns[0], kns[1])
            v_out[0] = jnp.where(first_half, v[:, 0:LANES], v[:, LANES:2 * LANES])

    def ctx_tile():
        write_mla_kv(cckv_ref[0], ckr_ref[0])
        k = cgk_ref[0]
        k_sw = pltpu.roll(k, GQA_HEAD_DIM, 1)
        kg_ref[0, :, 0:LANES] = jnp.where(first_half, k, k_sw).astype(BF16)
        kg_ref[0, :, LANES:2 * LANES] = jnp.where(first_half, k_sw, k).astype(BF16)
        v = cgv_ref[0]
        vg_ref[0, :, 0:LANES] = jnp.where(first_half, v, ones_lane).astype(BF16)
        vg_ref[0, :, LANES:2 * LANES] = jnp.where(first_half, pltpu.roll(v, GQA_HEAD_DIM, 1), ones_lane).astype(BF16)

    if nct:
        pl.when(t < nct)(ctx_tile)
        pl.when(t >= nct)(new_tile)
    else:
        new_tile()


def _attention_operands(proj, wts, tables, ctx, layer, emit_cache, tm):
    b, l, _ = proj.shape
    has_pos = tables is not None
    nct = 0 if ctx is None else ctx[0].shape[2] // tm
    lk = l + nct * tm
    nt = lk // tm

    def new_t(t):
        return jnp.maximum(t - nct, 0) if nct else t

    in_specs = [
        pl.BlockSpec((1, tm, 512), lambda i, t: (i, new_t(t), COL_MLA // 512)),
        pl.BlockSpec((1, tm, 1024), lambda i, t: (i, new_t(t), COL_GQA // 1024)),
    ] + [_const_spec(w.shape) for w in wts]
    args = [proj, proj] + list(wts)
    if has_pos:
        in_specs += [pl.BlockSpec((tm, LANES), lambda i, t: (new_t(t), 0))] * 4
        args += list(tables)
    if nct:
        in_specs += [pl.BlockSpec((1, None, tm, LANES), lambda i, t: (i, layer, jnp.minimum(t, nct - 1), 0))] * 4
        args += list(ctx)
    q_spec = lambda w: pl.BlockSpec((1, tm, w), lambda i, t: (i, new_t(t), 0))
    k_spec = lambda w: pl.BlockSpec((1, tm, w), lambda i, t: (i, t, 0))
    out_specs = [q_spec(512), k_spec(512), k_spec(512), q_spec(512), k_spec(256), k_spec(256)]
    out_shape = [
        jax.ShapeDtypeStruct((b, l, 512), BF16), jax.ShapeDtypeStruct((b, lk, 512), BF16),
        jax.ShapeDtypeStruct((b, lk, 512), BF16), jax.ShapeDtypeStruct((b, l, 512), BF16),
        jax.ShapeDtypeStruct((b, lk, 256), BF16), jax.ShapeDtypeStruct((b, lk, 256), BF16),
    ]
    if emit_cache:
        out_specs += [q_spec(MLA_KV_LORA), q_spec(MLA_ROPE), q_spec(LANES), q_spec(LANES)]
        out_shape += [
            jax.ShapeDtypeStruct((b, l, MLA_KV_LORA), F32), jax.ShapeDtypeStruct((b, l, MLA_ROPE), F32),
            jax.ShapeDtypeStruct((b, l, LANES), F32), jax.ShapeDtypeStruct((b, l, LANES), F32),
        ]
    return pl.pallas_call(
        functools.partial(_qkv_kernel, nct=nct, has_pos=has_pos, emit_cache=emit_cache),
        grid=(b, nt),
        in_specs=in_specs,
        out_specs=out_specs,
        out_shape=out_shape,
        compiler_params=_cparams(2),
        name="attention_operands",
    )(*args)


def _attn_kernel(q_ref, k_ref, v_ref, o_ref, *, shared_kv):
    lane = lax.broadcasted_iota(jnp.int32, (1, LANES), 1)
    for i in range(q_ref.shape[0]):
        outs = []
        for a in range(2):
            cols = slice(0, LANES) if shared_kv else slice(a * LANES, (a + 1) * LANES)
            qa = q_ref[i, :, a * LANES:(a + 1) * LANES]
            s = lax.dot_general(qa, k_ref[i, :, cols], NT_DIMS, preferred_element_type=F32)
            p = jnp.exp2(s - jnp.max(s, axis=-1, keepdims=True)).astype(BF16)
            acc = _dot(p, v_ref[i, :, cols])
            outs.append(acc / acc[:, GQA_HEAD_DIM:GQA_HEAD_DIM + 1])
        o_ref[i] = jnp.where(lane < GQA_HEAD_DIM, outs[0], pltpu.roll(outs[1], GQA_HEAD_DIM, 1)).astype(BF16)


def _attention(q, k, v, shared_kv, tq, nb):
    b, l, _ = q.shape
    lk = k.shape[1]
    kw = LANES if shared_kv else 2 * LANES
    return pl.pallas_call(
        functools.partial(_attn_kernel, shared_kv=shared_kv),
        grid=(b // nb, 2, l // tq),
        in_specs=[
            pl.BlockSpec((nb, tq, 2 * LANES), lambda i, j, t: (i, t, j)),
            pl.BlockSpec((nb, lk, kw), lambda i, j, t: (i, 0, j)),
            pl.BlockSpec((nb, lk, kw), lambda i, j, t: (i, 0, j)),
        ],
        out_specs=pl.BlockSpec((nb, tq, LANES), lambda i, j, t: (i, t, j)),
        out_shape=jax.ShapeDtypeStruct((b, l, 2 * LANES), BF16),
        compiler_params=_cparams(3),
        name="attention_shared_kv" if shared_kv else "attention",
    )(q, k, v)


def _post_kernel(*refs, moe):
    it = iter(refs)
    ys_ref, z_ref, ym_ref, yg_ref, x_ref, g1_ref, sh2_ref, sc2_ref = (next(it) for _ in range(8))
    ng_ref, gpost_ref, gpre_ref, wout_ref = (next(it) for _ in range(4))
    wr_ref = next(it) if moe else None
    x1_ref, h2_ref = next(it), next(it)
    comb_ref, combt_ref = (next(it), next(it)) if moe else (None, None)

    y = _rms(ys_ref[0] * _silu(z_ref[0].astype(F32))) * ng_ref[...]
    cat = jnp.concatenate([y.astype(BF16), ym_ref[0], yg_ref[0]], axis=1)
    mix = _dot(cat, wout_ref[...])
    x1 = x_ref[0] + g1_ref[0] * (_rms(mix) * gpost_ref[...])
    x1_ref[0] = x1
    h2 = _rms(x1) * gpre_ref[...] * (1.0 + sc2_ref[0]) + sh2_ref[0]
    h16 = h2.astype(BF16)
    h2_ref[0] = h16
    if moe:
        w_hi = wr_ref[...].astype(BF16)
        w_lo = (wr_ref[...] - w_hi.astype(F32)).astype(BF16)
        logits = _dot(h16, w_hi) + _dot(h16, w_lo)
        lane = lax.broadcasted_iota(jnp.int32, (1, LANES), 1).astype(F32)
        lg = jnp.where(lane < N_EXPERTS, logits, -jnp.inf)
        m1 = jnp.max(lg, axis=-1, keepdims=True)
        i1 = jnp.min(jnp.where(lg == m1, lane, float(LANES)), axis=-1, keepdims=True)
        lg2 = jnp.where(lane == i1, -jnp.inf, lg)
        m2 = jnp.max(lg2, axis=-1, keepdims=True)
        i2 = jnp.min(jnp.where(lg2 == m2, lane, float(LANES)), axis=-1, keepdims=True)
        e = jnp.exp(m2 - m1)
        comb = jnp.where(lane == i1, 1.0 / (1.0 + e), jnp.where(lane == i2, e / (1.0 + e), 0.0))
        comb_ref[0] = comb
        combt_ref[...] = comb.T[0:N_EXPERTS, :]


def _post_attention(yssd, proj, ym, yg, x, mod, mod_row, ng, gpost, gpre, w_out, w_router, tm):
    b, l, d = x.shape
    moe = w_router is not None
    tok = lambda w: pl.BlockSpec((1, tm, w), lambda i, t: (i, t, 0))
    modk = lambda k: pl.BlockSpec((1, 1, d), lambda i, t: (mod_row(i), 0, k))
    in_specs = [tok(SSD_INNER), tok(SSD_INNER), tok(256), tok(256), tok(d), modk(2), modk(3), modk(4),
                _const_spec(ng.shape), _const_spec(gpost.shape), _const_spec(gpre.shape),
                _const_spec(w_out.shape)]
    args = [yssd, proj, ym, yg, x, mod, mod, mod, ng, gpost, gpre, w_out]
    out_specs = [tok(d), tok(d)]
    out_shape = [jax.ShapeDtypeStruct((b, l, d), F32), jax.ShapeDtypeStruct((b, l, d), BF16)]
    if moe:
        in_specs.append(_const_spec(w_router.shape))
        args.append(w_router)
        out_specs += [tok(LANES), pl.BlockSpec((N_EXPERTS, tm), lambda i, t: (0, i * (l // tm) + t))]
        out_shape += [jax.ShapeDtypeStruct((b, l, LANES), F32), jax.ShapeDtypeStruct((N_EXPERTS, b * l), F32)]
    outs = pl.pallas_call(
        functools.partial(_post_kernel, moe=moe),
        grid=(b, l // tm),
        in_specs=in_specs,
        out_specs=out_specs,
        out_shape=out_shape,
        compiler_params=_cparams(2),
        name="post_attention",
    )(*args)
    return outs if moe else (outs[0], outs[1], None, None)


def _ffn_kernel(h_ref, x1_ref, g2_ref, gp_ref, wg_ref, wu_ref, wd_ref, o_ref):
    h = h_ref[0]
    act = (_silu(_dot(h, wg_ref[...])) * _dot(h, wu_ref[...])).astype(BF16)
    f = _dot(act, wd_ref[...])
    o_ref[0] = x1_ref[0] + g2_ref[0] * (_rms(f) * gp_ref[...])


def _dense_ffn(h2, x1, mod, mod_row, gp, wg, wu, wd, tm):
    b, l, d = x1.shape
    tok = pl.BlockSpec((1, tm, d), lambda i, t: (i, t, 0))
    return pl.pallas_call(
        _ffn_kernel,
        grid=(b, l // tm),
        in_specs=[tok, tok, pl.BlockSpec((1, 1, d), lambda i, t: (mod_row(i), 0, 5)),
                  _const_spec(gp.shape), _const_spec(wg.shape), _const_spec(wu.shape), _const_spec(wd.shape)],
        out_specs=tok,
        out_shape=jax.ShapeDtypeStruct((b, l, d), F32),
        compiler_params=_cparams(2),
        name="dense_ffn",
    )(h2, x1, mod, gp, wg, wu, wd)


MOE_BLOCK = 1024
MOE_CAP = 288
MOE_SUPER = 2048


def _moe_kernel(h_ref, comb_ref, combt_ref, wg_ref, wu_ref, wd_ref, o_ref, pos_s, *, tb, cap):
    e = pl.program_id(1)
    nsub = o_ref.shape[0] // tb

    @pl.when(e == 0)
    def _():
        o_ref[...] = jnp.zeros(o_ref.shape, F32)
        si = lax.broadcasted_iota(jnp.int32, (tb, tb), 0)
        ti = lax.broadcasted_iota(jnp.int32, (tb, tb), 1)
        before = jnp.where(si < ti, 1.0, 0.0).astype(BF16)
        for u in range(nsub):
            sel = jnp.where(combt_ref[:, u * tb:(u + 1) * tb] > 0.0, 1.0, 0.0).astype(BF16)
            pos_s[:, u * tb:(u + 1) * tb] = _dot(sel, before)

    lane = lax.broadcasted_iota(jnp.int32, (1, LANES), 1)
    slot = lax.broadcasted_iota(jnp.int32, (cap, 1), 0).astype(F32)

    def block(u, carry):
        off = pl.multiple_of(u * tb, tb)
        sel_e = combt_ref[pl.ds(e, 1), pl.ds(off, tb)] > 0.0
        pos_e = pos_s[pl.ds(e, 1), pl.ds(off, tb)]
        n_e = jnp.sum(jnp.where(sel_e, 1.0, 0.0)).astype(jnp.int32)
        gate = jnp.sum(jnp.where(lane == e, comb_ref[pl.ds(off, tb), :], 0.0), axis=-1, keepdims=True)

        def tile(j, carry):
            first = j * float(cap)
            pick = jnp.where(sel_e & (pos_e - first == slot), 1.0, 0.0).astype(BF16)
            xg = _dot(pick, h_ref[pl.ds(off, tb), :]).astype(BF16)
            act = (_silu(_dot(xg, wg_ref[0])) * _dot(xg, wu_ref[0])).astype(BF16)
            y = _dot(act, wd_ref[0]).astype(BF16)
            o_ref[pl.ds(off, tb), :] += gate * lax.dot_general(pick, y, TN_DIMS, preferred_element_type=F32)
            return carry

        lax.fori_loop(0, (n_e + cap - 1) // cap, tile, 0)
        return carry

    lax.fori_loop(0, nsub, block, 0)


def _moe_ffn(h2, comb, combt, wg, wu, wd, tb, sup):
    t, d = h2.shape
    n_e, _, ff = wg.shape
    cap = -(-(MOE_CAP * tb // MOE_BLOCK) // 16) * 16
    once = pl.Buffered(1)
    return pl.pallas_call(
        functools.partial(_moe_kernel, tb=tb, cap=cap),
        grid=(t // sup, n_e),
        in_specs=[pl.BlockSpec((sup, d), lambda s, e: (s, 0), pipeline_mode=once),
                  pl.BlockSpec((sup, LANES), lambda s, e: (s, 0), pipeline_mode=once),
                  pl.BlockSpec((N_EXPERTS, sup), lambda s, e: (0, s), pipeline_mode=once),
                  pl.BlockSpec((1, d, ff), lambda s, e: (e, 0, 0)),
                  pl.BlockSpec((1, d, ff), lambda s, e: (e, 0, 0)),
                  pl.BlockSpec((1, ff, d), lambda s, e: (e, 0, 0))],
        out_specs=pl.BlockSpec((sup, d), lambda s, e: (s, 0), pipeline_mode=once),
        out_shape=jax.ShapeDtypeStruct((t, d), F32),
        scratch_shapes=[pltpu.VMEM((N_EXPERTS, sup), F32)],
        compiler_params=_cparams(2),
        name="moe_ffn",
    )(h2, comb, combt, wg, wu, wd)


def _residual_kernel(f_ref, x1_ref, g2_ref, gp_ref, o_ref):
    o_ref[0] = x1_ref[0] + g2_ref[0] * (_rms(f_ref[0]) * gp_ref[...])


def _gated_residual(f, x1, mod, mod_row, gp, tm):
    b, l, d = x1.shape
    tok = pl.BlockSpec((1, tm, d), lambda i, t: (i, t, 0))
    return pl.pallas_call(
        _residual_kernel,
        grid=(b, l // tm),
        in_specs=[tok, tok, pl.BlockSpec((1, 1, d), lambda i, t: (mod_row(i), 0, 5)), _const_spec(gp.shape)],
        out_specs=tok,
        out_shape=jax.ShapeDtypeStruct((b, l, d), F32),
        compiler_params=_cparams(2),
        name="gated_residual",
    )(f, x1, mod, gp)


def _in_proj_columns():
    o_dt = 1536
    o_ql, o_kv, o_kpe = 1552, 1808, 1936
    o_gq, o_gk, o_gv = 1968, 2224, 2352
    idx = np.full((IN_COLS_PADDED,), -1, np.int64)
    idx[0:1536] = np.arange(1536)
    idx[COL_MLA:COL_MLA + 256] = o_ql + np.arange(256)
    idx[COL_MLA + 256:COL_MLA + 384] = o_kv + np.arange(128)
    idx[COL_MLA + 384 + MLA_NOPE:COL_MLA + 384 + MLA_NOPE + MLA_ROPE] = o_kpe + np.arange(MLA_ROPE)
    for h in range(GQA_HEADS):
        idx[COL_GQA + h * LANES:COL_GQA + h * LANES + 64] = o_gq + h * 64 + np.arange(64)
    for j in range(GQA_KV_HEADS):
        for r in range(2):
            c0 = COL_GQA + 512 + j * LANES + r * 64
            idx[c0:c0 + 64] = o_gk + j * 64 + np.arange(64)
            idx[c0 + 256:c0 + 256 + 64] = o_gv + j * 64 + np.arange(64)
    for g in range(SSD_GROUPS):
        for d in range(2):
            c0 = COL_DT + g * LANES + d * HEADS_PER_GROUP
            idx[c0:c0 + HEADS_PER_GROUP] = o_dt + d * SSD_HEADS + g * HEADS_PER_GROUP + np.arange(HEADS_PER_GROUP)
    return idx


def _gather_cols(w, idx):
    cols = jnp.take(w, jnp.asarray(np.maximum(idx, 0)), axis=1)
    return jnp.where(jnp.asarray(idx >= 0)[None, :], cols, 0.0)


def _mla_weight_columns():
    per_q = MLA_NOPE + MLA_ROPE
    uq = np.full((MLA_HEADS * LANES,), -1, np.int64)
    uk = np.full((MLA_HEADS * LANES,), -1, np.int64)
    uv = np.full((MLA_HEADS * LANES,), -1, np.int64)
    for h in range(MLA_HEADS):
        uq[h * LANES:h * LANES + per_q] = h * per_q + np.arange(per_q)
        uk[h * LANES:h * LANES + MLA_NOPE] = h * (MLA_NOPE + MLA_V) + np.arange(MLA_NOPE)
        uv[h * LANES:h * LANES + MLA_V] = h * (MLA_NOPE + MLA_V) + MLA_NOPE + np.arange(MLA_V)
    return uq, uk, uv


def _rope_tables(n_tokens):
    t = np.arange(n_tokens)
    row, colp = (t // GRID_W).astype(np.float64), (t % GRID_W).astype(np.float64)

    def table(width, lane0, head_dim, reps):
        cos = np.ones((n_tokens, width))
        sin = np.zeros((n_tokens, width))
        half = head_dim // 2
        quarter = half // 2
        for dim in range(head_dim):
            pos = row if dim < half else colp
            inv = ROPE_BASE ** (-(2.0 * (dim % quarter)) / half)
            ang = pos * inv
            sign = -1.0 if (dim % half) < quarter else 1.0
            for r in range(reps):
                cos[:, lane0 + r * head_dim + dim] = np.cos(ang)
                sin[:, lane0 + r * head_dim + dim] = sign * np.sin(ang)
        return jnp.asarray(cos, F32), jnp.asarray(sin, F32)

    mcos, msin = table(LANES, MLA_NOPE, MLA_ROPE, 1)
    gcos, gsin = table(LANES, 0, GQA_HEAD_DIM, 2)
    return mcos, msin, gcos, gsin


def _trunk_layer(i, x, mod_i, mod_row, p, tables, ctx):
    b, l, d = x.shape
    tm = min(512, l)
    emit = ctx is None
    proj, dt = _in_projection(x, mod_i, mod_row, p["g_mix_pre"][i], p["w_in"][i], tm)
    h0 = None if ctx is None else ctx["state"][:, i]
    yssd, state = _ssd_mixer(proj, dt, p["conv_w"][i], p["conv_b"][i], p["ssd_par"][i], h0, emit)
    ctx_kv = None if ctx is None else ctx["kv"]
    ops = _attention_operands(proj, p["attn_w"][i], tables, ctx_kv, i, emit, tm)
    qm, km, vm, qg, kg, vg = ops[:6]
    tq = min(256, l)
    nb = math.gcd(b, max(1, 1024 // km.shape[1]))
    ym = _attention(qm, km, vm, False, tq, nb)
    yg = _attention(qg, kg, vg, True, tq, nb)
    moe = i % 2 == 1
    j = i // 2
    x1, h2, comb, combt = _post_attention(
        yssd, proj, ym, yg, x, mod_i, mod_row, p["ssd_norm_g"][i], p["g_mix_post"][i], p["g_ffn_pre"][i],
        p["w_out"][i], p["w_router"][j] if moe else None, tm)
    if moe:
        sup = min(MOE_SUPER, b * l)
        f = _moe_ffn(h2.reshape(b * l, d), comb.reshape(b * l, LANES), combt,
                     p["moe_wg"][j], p["moe_wu"][j], p["moe_wd"][j], min(MOE_BLOCK, sup), sup)
        x2 = _gated_residual(f.reshape(b, l, d), x1, mod_i, mod_row, p["g_ffn_post"][i], tm)
    else:
        x2 = _dense_ffn(h2, x1, mod_i, mod_row, p["g_ffn_post"][i],
                        p["ffn_wg"][j], p["ffn_wu"][j], p["ffn_wd"][j], tm)
    return x2, (state,) + tuple(ops[6:])


def kernel(x_prompt, x_sample, state_ssd, cache_mla_ckv, cache_mla_krope, cache_gqa_k, cache_gqa_v, c, c_ctx, w_mod, b_mod, g_mix_pre, g_mix_post, g_ffn_pre, g_ffn_post, w_in, ssd_conv_w, ssd_conv_b, ssd_A_log, ssd_dt_bias, ssd_D, ssd_norm_g, mla_q_norm_g, mla_w_uq, mla_kv_norm_g, mla_w_ukv, gqa_q_norm_g, gqa_k_norm_g, w_out, ffn_w_gate, ffn_w_up, ffn_w_down, moe_w_router, moe_w_gate, moe_w_up, moe_w_down):
    depth = w_in.shape[0]
    d = x_prompt.shape[-1]
    n_dec = x_sample.shape[0]
    hpg = HEADS_PER_GROUP

    in_idx = _in_proj_columns()
    uq_idx, uk_idx, uv_idx = _mla_weight_columns()
    row2 = lambda a: a.reshape(depth, 1, a.shape[-1])
    def ssd_rows(fwd, bwd):
        cols = [fwd.reshape(depth, SSD_GROUPS, hpg), bwd.reshape(depth, SSD_GROUPS, hpg),
                jnp.zeros((depth, SSD_GROUPS, SSD_ROWS - 2 * hpg), F32)]
        return jnp.concatenate(cols, axis=-1)

    par = jnp.concatenate([ssd_rows(ssd_dt_bias[:, 0], ssd_dt_bias[:, 1]),
                           ssd_rows(ssd_A_log[:, 0], ssd_A_log[:, 1]),
                           ssd_rows(ssd_D, jnp.zeros_like(ssd_D))], axis=-1)
    par = jnp.broadcast_to(par[..., None], par.shape + (LANES,))
    zeros64 = jnp.zeros((depth, 1, GQA_HEAD_DIM), F32)
    attn_w = []
    for i in range(depth):
        attn_w.append((
            mla_q_norm_g[i][None, :], mla_kv_norm_g[i][None, :],
            _gather_cols(mla_w_uq[i], uq_idx).astype(BF16),
            _gather_cols(mla_w_ukv[i], uk_idx).astype(BF16),
            _gather_cols(mla_w_ukv[i], uv_idx).astype(BF16),
            jnp.concatenate([gqa_q_norm_g[i][None, :], zeros64[i]], axis=1),
            jnp.concatenate([gqa_k_norm_g[i][None, :]] * 2, axis=1),
        ))
    p = dict(
        g_mix_pre=row2(g_mix_pre), g_mix_post=row2(g_mix_post), g_ffn_pre=row2(g_ffn_pre),
        g_ffn_post=row2(g_ffn_post), ssd_norm_g=row2(ssd_norm_g),
        w_in=[_gather_cols(w_in[i], in_idx).astype(BF16) for i in range(depth)],
        conv_w=ssd_conv_w, conv_b=row2(ssd_conv_b), ssd_par=par, attn_w=attn_w,
        w_out=w_out.astype(BF16),
        ffn_wg=ffn_w_gate.astype(BF16), ffn_wu=ffn_w_up.astype(BF16), ffn_wd=ffn_w_down.astype(BF16),
        w_router=jnp.pad(moe_w_router, ((0, 0), (0, 0), (0, LANES - N_EXPERTS))),
        moe_wg=moe_w_gate.astype(BF16), moe_wu=moe_w_up.astype(BF16), moe_wd=moe_w_down.astype(BF16),
    )

    rows = -(-(1 + n_dec) // 8) * 8
    c_all = jnp.concatenate([c_ctx[None, :], c, jnp.zeros((rows - 1 - n_dec, d), F32)], axis=0)
    mod = _modulation(c_all, w_mod, b_mod).reshape(depth, rows, 1, 6 * d)

    y = x_prompt
    collected = [[], [], [], [], []]
    for i in range(depth):
        y, outs = _trunk_layer(i, y, mod[i], lambda bi: 0, p, None, None)
        for lst, t in zip(collected, outs):
            lst.append(t)
    y_prompt = y
    bsz, seq = x_prompt.shape[:2]
    new_state = jnp.stack(collected[0], axis=1)
    new_ckv = jnp.stack(collected[1], axis=1)
    new_krope = jnp.stack(collected[2], axis=1)
    new_k = jnp.stack(collected[3], axis=1).reshape(bsz, depth, seq, GQA_KV_HEADS, GQA_HEAD_DIM)
    new_v = jnp.stack(collected[4], axis=1).reshape(bsz, depth, seq, GQA_KV_HEADS, GQA_HEAD_DIM)

    past = cache_mla_ckv.shape[2]
    krope_blk = jnp.pad(cache_mla_krope, ((0, 0), (0, 0), (0, 0), (MLA_NOPE, LANES - MLA_NOPE - MLA_ROPE)))
    ctx = dict(
        state=state_ssd,
        kv=(cache_mla_ckv, krope_blk,
            cache_gqa_k.reshape(n_dec, depth, past, LANES), cache_gqa_v.reshape(n_dec, depth, past, LANES)),
    )
    tables = _rope_tables(x_sample.shape[1])
    y = x_sample
    for i in range(depth):
        y, _ = _trunk_layer(i, y, mod[i], lambda bi: bi + 1, p, tables, ctx)
    return (y_prompt, y, new_state, new_ckv, new_krope, new_k, new_v)
```

```python
import functools
import itertools
import math

import numpy as np
import jax
import jax.numpy as jnp
from jax import lax
from jax.experimental import pallas as pl
from jax.experimental.pallas import tpu as pltpu

F32 = jnp.float32
BF16 = jnp.bfloat16

EPS = 1e-6
ROPE_BASE = 10000.0
GRID_W = 64
SSD_HEAD_DIM = 64
SSD_HEADS = 8
SSD_GROUPS = 2
SSD_STATE = 128
SSD_CHUNK = 128
SSD_INNER = SSD_HEADS * SSD_HEAD_DIM
HEADS_PER_GROUP = SSD_HEADS // SSD_GROUPS
SSD_ROWS = 16
MLA_HEADS = 4
MLA_V = 64
MLA_NOPE = 64
MLA_ROPE = 32
MLA_Q_LORA = 256
MLA_KV_LORA = 128
GQA_HEADS = 4
GQA_KV_HEADS = 2
GQA_HEAD_DIM = 64
N_EXPERTS = 8

LANES = 128
BF16_SUBLANES = 16

COL_Z = 0
COL_X = 512
COL_B = 1024
COL_C = 1280
COL_MLA = 1536
COL_GQA = 2048
COL_DT = 3072
IN_COLS_PADDED = 3328

VMEM_LIMIT = 56 * 1024 * 1024

NT_DIMS = (((1,), (1,)), ((), ()))
TN_DIMS = (((0,), (0,)), ((), ()))


def _cparams(n_grid):
    return pltpu.CompilerParams(
        dimension_semantics=("arbitrary",) * n_grid, vmem_limit_bytes=VMEM_LIMIT)


def _const_spec(shape):
    nd = len(shape)
    return pl.BlockSpec(shape, lambda *_: (0,) * nd, pipeline_mode=pl.Buffered(1))


def _dot(a, b):
    return jnp.dot(a, b, preferred_element_type=F32)


def _rms(x, width=None):
    n = x.shape[-1] if width is None else width
    return x * lax.rsqrt(jnp.sum(x * x, axis=-1, keepdims=True) * (1.0 / n) + EPS)


def _silu(x):
    return x * jax.nn.sigmoid(x)


def _split3(v):
    hi = v.astype(BF16)
    r = v - hi.astype(F32)
    mid = r.astype(BF16)
    lo = (r - mid.astype(F32)).astype(BF16)
    return hi, mid, lo


def _mod_kernel(c_ref, w_ref, b_ref, o_ref):
    s = _silu(c_ref[...]).astype(BF16)
    o_ref[0] = _dot(s, w_ref[0].astype(BF16)) + b_ref[0]


def _modulation(c_all, w_mod, b_mod):
    depth, d, n = w_mod.shape
    tn = 1536
    rows = c_all.shape[0]
    return pl.pallas_call(
        _mod_kernel,
        grid=(depth, n // tn),
        in_specs=[
            pl.BlockSpec((rows, d), lambda i, j: (0, 0)),
            pl.BlockSpec((1, d, tn), lambda i, j: (i, 0, j)),
            pl.BlockSpec((1, 1, tn), lambda i, j: (i, 0, j)),
        ],
        out_specs=pl.BlockSpec((1, rows, tn), lambda i, j: (i, 0, j)),
        out_shape=jax.ShapeDtypeStruct((depth, rows, n), F32),
        compiler_params=_cparams(2),
        name="modulation",
    )(c_all, w_mod, b_mod.reshape(depth, 1, n))


def _inproj_kernel(x_ref, sh_ref, sc_ref, g_ref, w_ref, o_ref, dt_ref):
    h = _rms(x_ref[0]) * g_ref[...]
    h = h * (1.0 + sc_ref[0]) + sh_ref[0]
    proj = _dot(h.astype(BF16), w_ref[...])
    o_ref[0] = proj[:, 0:COL_DT].astype(BF16)
    dt_ref[0] = proj[:, COL_DT:]


def _in_projection(x, mod, mod_row, gain, w_in_p, tm):
    b, l, d = x.shape
    n = w_in_p.shape[1]
    n_dt = n - COL_DT
    return pl.pallas_call(
        _inproj_kernel,
        grid=(b, l // tm),
        in_specs=[
            pl.BlockSpec((1, tm, d), lambda i, t: (i, t, 0)),
            pl.BlockSpec((1, 1, d), lambda i, t: (mod_row(i), 0, 0)),
            pl.BlockSpec((1, 1, d), lambda i, t: (mod_row(i), 0, 1)),
            _const_spec((1, d)),
            _const_spec((d, n)),
        ],
        out_specs=[pl.BlockSpec((1, tm, COL_DT), lambda i, t: (i, t, 0)),
                   pl.BlockSpec((1, tm, n_dt), lambda i, t: (i, t, 0))],
        out_shape=[jax.ShapeDtypeStruct((b, l, COL_DT), BF16), jax.ShapeDtypeStruct((b, l, n_dt), F32)],
        compiler_params=_cparams(2),
        name="in_projection",
    )(x, mod, mod, gain, w_in_p)


def _ssd_kernel(*refs, nc, seq, has_h0, emit_state):
    it = iter(refs)
    x_ref, b_ref, c_ref, dt_ref = next(it), next(it), next(it), next(it)
    cwx_ref, cwb_ref, cwc_ref = next(it), next(it), next(it)
    cbx_ref, cbb_ref, cbc_ref = next(it), next(it), next(it)
    par_ref = next(it)
    h0_ref = next(it) if has_h0 else None
    y_ref = next(it)
    st_ref = next(it) if emit_state else None
    xt_s, yd_s, bc_s, dtv_s, arow_s, acol_s, h_s = (next(it) for _ in range(7))

    q = SSD_CHUNK
    hpg = HEADS_PER_GROUP
    hd = SSD_HEAD_DIM
    wx = hpg * hd
    r = SSD_ROWS
    pack = BF16_SUBLANES
    row = lax.broadcasted_iota(jnp.int32, (q, 1), 0)
    si = lax.broadcasted_iota(jnp.int32, (q, q), 0)
    ti = lax.broadcasted_iota(jnp.int32, (q, q), 1)
    le = si <= ti
    ge = si >= ti
    tri_le = jnp.where(le, 1.0, 0.0).astype(BF16)
    tri_ge = jnp.where(ge, 1.0, 0.0).astype(BF16)
    par = par_ref[0]
    bias_t, alog_t, d_t = par[0:r], par[r:2 * r], par[2 * r:3 * r]

    for d in range(2):
        if has_h0:
            h_s[d] = h0_ref[0, d].reshape(wx, SSD_STATE)
        else:
            h_s[d] = jnp.zeros((wx, SSD_STATE), F32)

    def conv_silu(ref, w_ref, bias_ref, c):
        off = pl.multiple_of(c * q, q)
        u = ref[0, pl.ds(off, q), :].astype(F32)
        before = ref[0, pl.ds(pl.multiple_of(jnp.maximum(off - pack, 0), pack), pack), :]
        after = ref[0, pl.ds(pl.multiple_of(jnp.minimum(off + q, seq - pack), pack), pack), :]
        prev = before[pack - 1:pack, :].astype(F32) * jnp.where(c > 0, 1.0, 0.0)
        nxt = after[0:1, :].astype(F32) * jnp.where(c < nc - 1, 1.0, 0.0)
        up = jnp.where(row == 0, prev, pltpu.roll(u, 1, 0))
        un = jnp.where(row == q - 1, nxt, pltpu.roll(u, q - 1, 0))
        w = w_ref[...]
        return _silu(up * w[0:1] + u * w[1:2] + un * w[2:3] + bias_ref[...])

    tri_rows = jnp.concatenate([tri_le, tri_ge], axis=1)
    tri_cols = jnp.concatenate([tri_ge, tri_le], axis=0)
    fwd_row = lax.broadcasted_iota(jnp.int32, (r, 1), 0) < hpg
    fwd_lane = lax.broadcasted_iota(jnp.int32, (1, r), 1) < hpg

    def prep_body(c, carry):
        off = pl.multiple_of(c * q, q)
        xt = conv_silu(x_ref, cwx_ref, cbx_ref, c).T
        xt_s[c] = xt
        bc_s[pl.ds(off, q), 0:SSD_STATE] = conv_silu(b_ref, cwb_ref, cbb_ref, c).astype(BF16)
        bc_s[pl.ds(off, q), SSD_STATE:2 * SSD_STATE] = conv_silu(c_ref, cwc_ref, cbc_ref, c).astype(BF16)
        dtr = dt_ref[0, pl.ds(off, q), :].T[0:r, :] + bias_t
        dtv = jnp.maximum(dtr, 0.0) + jnp.log1p(jnp.exp(-jnp.abs(dtr)))
        dtv_s[c] = dtv
        parts = _split3(dtv * (-jnp.exp(alog_t)))
        rows = sum(_dot(part, tri_rows) for part in parts)
        cols = sum(lax.dot_general(tri_cols, part, NT_DIMS, preferred_element_type=F32) for part in parts)
        arow_s[c] = jnp.where(fwd_row, rows[:, 0:q], rows[:, q:2 * q])
        acol_s[c] = jnp.where(fwd_lane, cols[0:q, :], cols[q:2 * q, :])
        return carry

    def chunk(c, d):
        off = pl.multiple_of(c * q, q)
        base, mask = (0, le) if d == 0 else (hpg, ge)
        xt = xt_s[c]
        b16 = bc_s[pl.ds(off, q), 0:SSD_STATE]
        c16 = bc_s[pl.ds(off, q), SSD_STATE:2 * SSD_STATE]
        dtv, a_rows, a_cols = dtv_s[c], arow_s[c], acol_s[c]
        tot = a_rows[:, q - 1:q] if d == 0 else a_rows[:, 0:1]
        st = lax.dot_general(b16, c16, NT_DIMS, preferred_element_type=F32)
        hs = h_s[d]
        yoff = lax.dot_general(hs.astype(BF16), c16, NT_DIMS, preferred_element_type=F32)
        yield
        outs, xdecs, hnew = [], [], []
        for hl in range(hpg):
            k = base + hl
            rs = slice(hl * hd, (hl + 1) * hd)
            a_t = a_rows[k:k + 1, :]
            tot_k = tot[k:k + 1, :]
            seg = a_t - a_cols[:, k:k + 1]
            w = (st * jnp.exp(jnp.where(mask, seg, -jnp.inf))).astype(BF16)
            xdt = xt[rs, :] * dtv[k:k + 1, :]
            outs.append(_dot(xdt.astype(BF16), w) + yoff[rs, :] * jnp.exp(a_t))
            xdecs.append((xdt * jnp.exp(tot_k - a_t)).astype(BF16))
            hnew.append(hs[rs, :] * jnp.exp(tot_k))
            yield
        h_s[d] = jnp.concatenate(hnew, axis=0) + _dot(jnp.concatenate(xdecs, axis=0), b16)
        yd_s[d, c] = jnp.concatenate(outs, axis=0)

    def scan_body(i, carry):
        for _ in itertools.zip_longest(chunk(i, 0), chunk(nc - 1 - i, 1)):
            pass
        return carry

    def out_body(c, carry):
        xt = xt_s[c]
        skip = [xt[hl * hd:(hl + 1) * hd, :] * d_t[hl:hl + 1, :] for hl in range(hpg)]
        yt = yd_s[0, c] + yd_s[1, c] + jnp.concatenate(skip, axis=0)
        y_ref[0, pl.ds(pl.multiple_of(c * q, q), q), :] = yt.T
        return carry

    lax.fori_loop(0, nc, prep_body, 0, unroll=2)
    lax.fori_loop(0, nc, scan_body, 0, unroll=2)
    lax.fori_loop(0, nc, out_body, 0, unroll=4)
    if emit_state:
        for d in range(2):
            st_ref[0, d] = h_s[d].reshape(hpg, SSD_HEAD_DIM, SSD_STATE)


def _ssd_mixer(proj, dt, conv_w, conv_b, par, h0, emit_state):
    b, l, _ = proj.shape
    g = SSD_GROUPS
    wx = HEADS_PER_GROUP * SSD_HEAD_DIM
    n = SSD_STATE
    nc = l // SSD_CHUNK
    has_h0 = h0 is not None

    def col(width, start):
        blk = start // width
        return pl.BlockSpec((1, l, width), lambda i, j: (i, 0, blk + j))

    def cw(width, start, rows):
        blk = start // width
        return pl.BlockSpec((rows, width), lambda i, j: (0, blk + j))

    in_specs = [
        col(wx, COL_X), col(n, COL_B), col(n, COL_C), col(LANES, 0),
        cw(wx, 0, 3), cw(n, SSD_INNER, 3), cw(n, SSD_INNER + g * n, 3),
        cw(wx, 0, 1), cw(n, SSD_INNER, 1), cw(n, SSD_INNER + g * n, 1),
        pl.BlockSpec((1, 3 * SSD_ROWS, LANES), lambda i, j: (j, 0, 0)),
    ]
    args = [proj, proj, proj, dt, conv_w, conv_w, conv_w, conv_b, conv_b, conv_b, par]
    state_spec = pl.BlockSpec((1, 2, HEADS_PER_GROUP, SSD_HEAD_DIM, n), lambda i, j: (i, 0, j, 0, 0))
    if has_h0:
        in_specs.append(state_spec)
        args.append(h0)
    out_specs = [pl.BlockSpec((1, l, wx), lambda i, j: (i, 0, j))]
    out_shape = [jax.ShapeDtypeStruct((b, l, SSD_INNER), F32)]
    if emit_state:
        out_specs.append(state_spec)
        out_shape.append(jax.ShapeDtypeStruct((b, 2, SSD_HEADS, SSD_HEAD_DIM, n), F32))
    outs = pl.pallas_call(
        functools.partial(_ssd_kernel, nc=nc, seq=l, has_h0=has_h0, emit_state=emit_state),
        grid=(b, g),
        in_specs=in_specs,
        out_specs=out_specs,
        out_shape=out_shape,
        scratch_shapes=[
            pltpu.VMEM((nc, wx, SSD_CHUNK), F32),
            pltpu.VMEM((2, nc, wx, SSD_CHUNK), F32),
            pltpu.VMEM((l, 2 * n), BF16),
            pltpu.VMEM((nc, SSD_ROWS, SSD_CHUNK), F32),
            pltpu.VMEM((nc, SSD_ROWS, SSD_CHUNK), F32),
            pltpu.VMEM((nc, SSD_CHUNK, SSD_ROWS), F32),
            pltpu.VMEM((2, wx, n), F32),
        ],
        compiler_params=_cparams(2),
        name="ssd_mixer",
    )(*args)
    return (outs[0], outs[1]) if emit_state else (outs[0], None)


def _rope(x, cos, sin_signed, pair):
    n = x.shape[1]
    lane = lax.broadcasted_iota(jnp.int32, (1, n), 1)
    first = (lane & (2 * pair - 1)) < pair
    partner = jnp.where(first, pltpu.roll(x, n - pair, 1), pltpu.roll(x, pair, 1))
    return x * cos + partner * sin_signed


def _tile_lanes(x, k):
    return jnp.concatenate([x] * k, axis=1)


def _qkv_kernel(*refs, nct, has_pos, emit_cache):
    it = iter(refs)
    mla_ref, gqa_ref = next(it), next(it)
    qg_ref, kvg_ref, wuq_ref, wk_ref, wv_ref, gq_ref, gk_ref = (next(it) for _ in range(7))
    if has_pos:
        mcos_ref, msin_ref, gcos_ref, gsin_ref = (next(it) for _ in range(4))
    if nct:
        cckv_ref, ckr_ref, cgk_ref, cgv_ref = (next(it) for _ in range(4))
    qm_ref, km_ref, vm_ref, qg_out, kg_ref, vg_ref = (next(it) for _ in range(6))
    if emit_cache:
        ckv_out, kpe_out, kn_out, v_out = (next(it) for _ in range(4))

    t = pl.program_id(1)
    lane = lax.broadcasted_iota(jnp.int32, (1, LANES), 1)
    first_half = lane < GQA_HEAD_DIM
    mla_scale = math.log2(math.e) / math.sqrt(MLA_NOPE + MLA_ROPE)
    gqa_scale = math.log2(math.e) / math.sqrt(GQA_HEAD_DIM)
    ones_lane = jnp.where(lane == GQA_HEAD_DIM, 1.0, 0.0)

    def write_mla_kv(ckv, kpe_r):
        c16 = ckv.astype(BF16)
        km_ref[0] = (_dot(c16, wk_ref[...]) + _tile_lanes(kpe_r, MLA_HEADS)).astype(BF16)
        vm_ref[0] = (_dot(c16, wv_ref[...]) + _tile_lanes(ones_lane, MLA_HEADS)).T.astype(BF16)

    def new_tile():
        m = mla_ref[0].astype(F32)
        q_lat = m[:, 0:MLA_Q_LORA]
        kv_lat = m[:, MLA_Q_LORA:MLA_Q_LORA + MLA_KV_LORA]
        kpe = m[:, MLA_Q_LORA + MLA_KV_LORA:]
        qn = _rms(q_lat) * qg_ref[...]
        qm = _dot(qn.astype(BF16), wuq_ref[...])
        ckv = _rms(kv_lat) * kvg_ref[...]
        kpe_r = kpe
        if has_pos:
            mcos, msin = mcos_ref[...], msin_ref[...]
            qm = _rope(qm, _tile_lanes(mcos, MLA_HEADS), _tile_lanes(msin, MLA_HEADS), MLA_ROPE // 4)
            kpe_r = _rope(kpe, mcos, msin, MLA_ROPE // 4)
        qm_ref[0] = (qm * mla_scale).astype(BF16)
        write_mla_kv(ckv, kpe_r)

        g = gqa_ref[0].astype(F32)
        if has_pos:
            gcos, gsin = gcos_ref[...], gsin_ref[...]
        for h in range(GQA_HEADS):
            xh = g[:, h * LANES:(h + 1) * LANES]
            qh = _rms(xh, GQA_HEAD_DIM) * gq_ref[...]
            if has_pos:
                qh = _rope(qh, gcos, gsin, GQA_HEAD_DIM // 4)
            qg_out[0, :, h * LANES:(h + 1) * LANES] = (qh * gqa_scale).astype(BF16)
        kns = []
        for j in range(GQA_KV_HEADS):
            c0 = GQA_HEADS * LANES + j * LANES
            kn = _rms(g[:, c0:c0 + LANES]) * gk_ref[...]
            kns.append(kn)
            kr = _rope(kn, gcos, gsin, GQA_HEAD_DIM // 4) if has_pos else kn
            kg_ref[0, :, j * LANES:(j + 1) * LANES] = kr.astype(BF16)
        v0 = (GQA_HEADS + GQA_KV_HEADS) * LANES
        v = g[:, v0:v0 + GQA_KV_HEADS * LANES]
        for j in range(GQA_KV_HEADS):
            vj = jnp.where(first_half, v[:, j * LANES:(j + 1) * LANES], ones_lane)
            vg_ref[0, j * LANES:(j + 1) * LANES, :] = vj.T.astype(BF16)
        if emit_cache:
            ckv_out[0] = ckv
            kpe_out[0] = kpe[:, MLA_NOPE:MLA_NOPE + MLA_ROPE]
            kn_out[0] = jnp.where(first_half, kns[0], kns[1])
            v_out[0] = jnp.where(first_half, v[:, 0:LANES], v[:, LANES:2 * LANES])

    def ctx_tile():
        write_mla_kv(cckv_ref[0], ckr_ref[0])
        k = cgk_ref[0]
        k_sw = pltpu.roll(k, GQA_HEAD_DIM, 1)
        kg_ref[0, :, 0:LANES] = jnp.where(first_half, k, k_sw).astype(BF16)
        kg_ref[0, :, LANES:2 * LANES] = jnp.where(first_half, k_sw, k).astype(BF16)
        v = cgv_ref[0]
        vg_ref[0, 0:LANES, :] = jnp.where(first_half, v, ones_lane).T.astype(BF16)
        vg_ref[0, LANES:2 * LANES, :] = jnp.where(first_half, pltpu.roll(v, GQA_HEAD_DIM, 1), ones_lane).T.astype(BF16)

    if nct:
        pl.when(t < nct)(ctx_tile)
        pl.when(t >= nct)(new_tile)
    else:
        new_tile()


def _attention_operands(proj, wts, tables, ctx, layer, emit_cache, tm):
    b, l, _ = proj.shape
    has_pos = tables is not None
    nct = 0 if ctx is None else ctx[0].shape[2] // tm
    lk = l + nct * tm
    nt = lk // tm

    def new_t(t):
        return jnp.maximum(t - nct, 0) if nct else t

    in_specs = [
        pl.BlockSpec((1, tm, 512), lambda i, t: (i, new_t(t), COL_MLA // 512)),
        pl.BlockSpec((1, tm, 1024), lambda i, t: (i, new_t(t), COL_GQA // 1024)),
    ] + [_const_spec(w.shape) for w in wts]
    args = [proj, proj] + list(wts)
    if has_pos:
        in_specs += [pl.BlockSpec((tm, LANES), lambda i, t: (new_t(t), 0))] * 4
        args += list(tables)
    if nct:
        in_specs += [pl.BlockSpec((1, None, tm, LANES), lambda i, t: (i, layer, jnp.minimum(t, nct - 1), 0))] * 4
        args += list(ctx)
    q_spec = lambda w: pl.BlockSpec((1, tm, w), lambda i, t: (i, new_t(t), 0))
    k_spec = lambda w: pl.BlockSpec((1, tm, w), lambda i, t: (i, t, 0))
    vt_spec = lambda w: pl.BlockSpec((1, w, tm), lambda i, t: (i, 0, t))
    out_specs = [q_spec(512), k_spec(512), vt_spec(512), q_spec(512), k_spec(256), vt_spec(256)]
    out_shape = [
        jax.ShapeDtypeStruct((b, l, 512), BF16), jax.ShapeDtypeStruct((b, lk, 512), BF16),
        jax.ShapeDtypeStruct((b, 512, lk), BF16), jax.ShapeDtypeStruct((b, l, 512), BF16),
        jax.ShapeDtypeStruct((b, lk, 256), BF16), jax.ShapeDtypeStruct((b, 256, lk), BF16),
    ]
    if emit_cache:
        out_specs += [q_spec(MLA_KV_LORA), q_spec(MLA_ROPE), q_spec(LANES), q_spec(LANES)]
        out_shape += [
            jax.ShapeDtypeStruct((b, l, MLA_KV_LORA), F32), jax.ShapeDtypeStruct((b, l, MLA_ROPE), F32),
            jax.ShapeDtypeStruct((b, l, LANES), F32), jax.ShapeDtypeStruct((b, l, LANES), F32),
        ]
    return pl.pallas_call(
        functools.partial(_qkv_kernel, nct=nct, has_pos=has_pos, emit_cache=emit_cache),
        grid=(b, nt),
        in_specs=in_specs,
        out_specs=out_specs,
        out_shape=out_shape,
        compiler_params=_cparams(2),
        name="attention_operands",
    )(*args)


def _attn_kernel(q_ref, k_ref, vt_ref, o_ref, *, shared_kv):
    lane = lax.broadcasted_iota(jnp.int32, (1, LANES), 1)
    heads = (0, 1)
    cols = [slice(0, LANES) if shared_kv else slice(a * LANES, (a + 1) * LANES) for a in heads]
    tsub = min(ATTN_SUBTILE, q_ref.shape[1])
    units = [(i, r) for i in range(q_ref.shape[0]) for r in range(0, q_ref.shape[1], tsub)]

    def scores(u):
        i, r = u
        return [lax.dot_general(k_ref[i, :, cols[a]], q_ref[i, r:r + tsub, a * LANES:(a + 1) * LANES], NT_DIMS,
                                preferred_element_type=F32) for a in heads]

    def finish(u, sts):
        i, r = u
        pts = []
        for st in sts:
            part = jnp.max(st.reshape(-1, 2 * LANES, st.shape[1]), axis=0)
            pts.append(jnp.exp2(st - jnp.max(part, axis=0, keepdims=True)).astype(BF16))
        accs = [_dot(vt_ref[i, cols[a], :], pts[a]) for a in heads]
        outs = [(acc / acc[GQA_HEAD_DIM:GQA_HEAD_DIM + 1, :]).T for acc in accs]
        o_ref[i, r:r + tsub, :] = jnp.where(
            lane < GQA_HEAD_DIM, outs[0], pltpu.roll(outs[1], GQA_HEAD_DIM, 1)).astype(BF16)

    sts = scores(units[0])
    for n, u in enumerate(units):
        nxt = scores(units[n + 1]) if n + 1 < len(units) else None
        finish(u, sts)
        sts = nxt


def _attention(q, k, v, shared_kv, tq, nb):
    b, l, _ = q.shape
    lk = k.shape[1]
    kw = LANES if shared_kv else 2 * LANES
    return pl.pallas_call(
        functools.partial(_attn_kernel, shared_kv=shared_kv),
        grid=(b // nb, 2, l // tq),
        in_specs=[
            pl.BlockSpec((nb, tq, 2 * LANES), lambda i, j, t: (i, t, j)),
            pl.BlockSpec((nb, lk, kw), lambda i, j, t: (i, 0, j)),
            pl.BlockSpec((nb, kw, lk), lambda i, j, t: (i, j, 0)),
        ],
        out_specs=pl.BlockSpec((nb, tq, LANES), lambda i, j, t: (i, t, j)),
        out_shape=jax.ShapeDtypeStruct((b, l, 2 * LANES), BF16),
        compiler_params=_cparams(3),
        name="attention_shared_kv" if shared_kv else "attention",
    )(q, k, v)


def _post_kernel(*refs, moe):
    it = iter(refs)
    ys_ref, z_ref, ym_ref, yg_ref, x_ref, g1_ref, sh2_ref, sc2_ref = (next(it) for _ in range(8))
    ng_ref, gpost_ref, gpre_ref, wout_ref = (next(it) for _ in range(4))
    wr_ref = next(it) if moe else None
    x1_ref, h2_ref = next(it), next(it)
    comb_ref, combt_ref = (next(it), next(it)) if moe else (None, None)

    y = _rms(ys_ref[0] * _silu(z_ref[0].astype(F32))) * ng_ref[...]
    cat = jnp.concatenate([y.astype(BF16), ym_ref[0], yg_ref[0]], axis=1)
    mix = _dot(cat, wout_ref[...])
    x1 = x_ref[0] + g1_ref[0] * (_rms(mix) * gpost_ref[...])
    x1_ref[0] = x1
    h2 = _rms(x1) * gpre_ref[...] * (1.0 + sc2_ref[0]) + sh2_ref[0]
    h16 = h2.astype(BF16)
    h2_ref[0] = h16
    if moe:
        w_hi = wr_ref[...].astype(BF16)
        w_lo = (wr_ref[...] - w_hi.astype(F32)).astype(BF16)
        logits = _dot(h16, w_hi) + _dot(h16, w_lo)
        lane = lax.broadcasted_iota(jnp.int32, (1, LANES), 1).astype(F32)
        lg = jnp.where(lane < N_EXPERTS, logits, -jnp.inf)
        m1 = jnp.max(lg, axis=-1, keepdims=True)
        i1 = jnp.min(jnp.where(lg == m1, lane, float(LANES)), axis=-1, keepdims=True)
        lg2 = jnp.where(lane == i1, -jnp.inf, lg)
        m2 = jnp.max(lg2, axis=-1, keepdims=True)
        i2 = jnp.min(jnp.where(lg2 == m2, lane, float(LANES)), axis=-1, keepdims=True)
        e = jnp.exp(m2 - m1)
        comb = jnp.where(lane == i1, 1.0 / (1.0 + e), jnp.where(lane == i2, e / (1.0 + e), 0.0))
        comb_ref[0] = comb
        combt_ref[...] = comb.T[0:N_EXPERTS, :]


def _post_attention(yssd, proj, ym, yg, x, mod, mod_row, ng, gpost, gpre, w_out, w_router, tm):
    b, l, d = x.shape
    moe = w_router is not None
    tok = lambda w: pl.BlockSpec((1, tm, w), lambda i, t: (i, t, 0))
    modk = lambda k: pl.BlockSpec((1, 1, d), lambda i, t: (mod_row(i), 0, k))
    in_specs = [tok(SSD_INNER), tok(SSD_INNER), tok(256), tok(256), tok(d), modk(2), modk(3), modk(4),
                _const_spec(ng.shape), _const_spec(gpost.shape), _const_spec(gpre.shape),
                _const_spec(w_out.shape)]
    args = [yssd, proj, ym, yg, x, mod, mod, mod, ng, gpost, gpre, w_out]
    out_specs = [tok(d), tok(d)]
    out_shape = [jax.ShapeDtypeStruct((b, l, d), F32), jax.ShapeDtypeStruct((b, l, d), BF16)]
    if moe:
        in_specs.append(_const_spec(w_router.shape))
        args.append(w_router)
        out_specs += [tok(LANES), pl.BlockSpec((N_EXPERTS, tm), lambda i, t: (0, i * (l // tm) + t))]
        out_shape += [jax.ShapeDtypeStruct((b, l, LANES), F32), jax.ShapeDtypeStruct((N_EXPERTS, b * l), F32)]
    outs = pl.pallas_call(
        functools.partial(_post_kernel, moe=moe),
        grid=(b, l // tm),
        in_specs=in_specs,
        out_specs=out_specs,
        out_shape=out_shape,
        compiler_params=_cparams(2),
        name="post_attention",
    )(*args)
    return outs if moe else (outs[0], outs[1], None, None)


def _ffn_kernel(h_ref, x1_ref, g2_ref, gp_ref, wg_ref, wu_ref, wd_ref, o_ref):
    h = h_ref[0]
    act = (_silu(_dot(h, wg_ref[...])) * _dot(h, wu_ref[...])).astype(BF16)
    f = _dot(act, wd_ref[...])
    o_ref[0] = x1_ref[0] + g2_ref[0] * (_rms(f) * gp_ref[...])


def _dense_ffn(h2, x1, mod, mod_row, gp, wg, wu, wd, tm):
    b, l, d = x1.shape
    tok = pl.BlockSpec((1, tm, d), lambda i, t: (i, t, 0))
    return pl.pallas_call(
        _ffn_kernel,
        grid=(b, l // tm),
        in_specs=[tok, tok, pl.BlockSpec((1, 1, d), lambda i, t: (mod_row(i), 0, 5)),
                  _const_spec(gp.shape), _const_spec(wg.shape), _const_spec(wu.shape), _const_spec(wd.shape)],
        out_specs=tok,
        out_shape=jax.ShapeDtypeStruct((b, l, d), F32),
        compiler_params=_cparams(2),
        name="dense_ffn",
    )(h2, x1, mod, gp, wg, wu, wd)


ATTN_TILE = 2048
ATTN_SUBTILE = 256

MOE_BLOCK = 1024
MOE_CAP = 256
MOE_SUPER = 2048


def _moe_kernel(h_ref, comb_ref, combt_ref, wg_ref, wu_ref, wd_ref, o_ref, pos_s, *, tb, cap):
    e = pl.program_id(1)
    nsub = o_ref.shape[0] // tb

    @pl.when(e == 0)
    def _():
        o_ref[...] = jnp.zeros(o_ref.shape, F32)
        si = lax.broadcasted_iota(jnp.int32, (tb, tb), 0)
        ti = lax.broadcasted_iota(jnp.int32, (tb, tb), 1)
        before = jnp.where(si < ti, 1.0, 0.0).astype(BF16)
        for u in range(nsub):
            sel = jnp.where(combt_ref[:, u * tb:(u + 1) * tb] > 0.0, 1.0, 0.0).astype(BF16)
            pos_s[:, u * tb:(u + 1) * tb] = _dot(sel, before)

    lane = lax.broadcasted_iota(jnp.int32, (1, LANES), 1)
    slot = lax.broadcasted_iota(jnp.int32, (cap, 1), 0).astype(F32)

    def block(u, carry):
        off = pl.multiple_of(u * tb, tb)
        sel_e = combt_ref[pl.ds(e, 1), pl.ds(off, tb)] > 0.0
        pos_e = pos_s[pl.ds(e, 1), pl.ds(off, tb)]
        n_e = jnp.sum(jnp.where(sel_e, 1.0, 0.0)).astype(jnp.int32)
        gate = jnp.sum(jnp.where(lane == e, comb_ref[pl.ds(off, tb), :], 0.0), axis=-1, keepdims=True)

        def tile(j, carry):
            first = j * float(cap)
            pick = jnp.where(sel_e & (pos_e - first == slot), 1.0, 0.0).astype(BF16)
            xg = _dot(pick, h_ref[pl.ds(off, tb), :]).astype(BF16)
            act = (_silu(_dot(xg, wg_ref[0])) * _dot(xg, wu_ref[0])).astype(BF16)
            y = _dot(act, wd_ref[0]).astype(BF16)
            o_ref[pl.ds(off, tb), :] += gate * lax.dot_general(pick, y, TN_DIMS, preferred_element_type=F32)
            return carry

        lax.fori_loop(0, (n_e + cap - 1) // cap, tile, 0)
        return carry

    lax.fori_loop(0, nsub, block, 0)


def _moe_ffn(h2, comb, combt, wg, wu, wd, tb, sup):
    t, d = h2.shape
    n_e, _, ff = wg.shape
    cap = -(-(MOE_CAP * tb // MOE_BLOCK) // 16) * 16
    once = pl.Buffered(1)
    return pl.pallas_call(
        functools.partial(_moe_kernel, tb=tb, cap=cap),
        grid=(t // sup, n_e),
        in_specs=[pl.BlockSpec((sup, d), lambda s, e: (s, 0), pipeline_mode=once),
                  pl.BlockSpec((sup, LANES), lambda s, e: (s, 0), pipeline_mode=once),
                  pl.BlockSpec((N_EXPERTS, sup), lambda s, e: (0, s), pipeline_mode=once),
                  pl.BlockSpec((1, d, ff), lambda s, e: (e, 0, 0)),
                  pl.BlockSpec((1, d, ff), lambda s, e: (e, 0, 0)),
                  pl.BlockSpec((1, ff, d), lambda s, e: (e, 0, 0))],
        out_specs=pl.BlockSpec((sup, d), lambda s, e: (s, 0), pipeline_mode=once),
        out_shape=jax.ShapeDtypeStruct((t, d), F32),
        scratch_shapes=[pltpu.VMEM((N_EXPERTS, sup), F32)],
        compiler_params=_cparams(2),
        name="moe_ffn",
    )(h2, comb, combt, wg, wu, wd)


def _residual_kernel(f_ref, x1_ref, g2_ref, gp_ref, o_ref):
    o_ref[0] = x1_ref[0] + g2_ref[0] * (_rms(f_ref[0]) * gp_ref[...])


def _gated_residual(f, x1, mod, mod_row, gp, tm):
    b, l, d = x1.shape
    tok = pl.BlockSpec((1, tm, d), lambda i, t: (i, t, 0))
    return pl.pallas_call(
        _residual_kernel,
        grid=(b, l // tm),
        in_specs=[tok, tok, pl.BlockSpec((1, 1, d), lambda i, t: (mod_row(i), 0, 5)), _const_spec(gp.shape)],
        out_specs=tok,
        out_shape=jax.ShapeDtypeStruct((b, l, d), F32),
        compiler_params=_cparams(2),
        name="gated_residual",
    )(f, x1, mod, gp)


def _in_proj_columns():
    o_dt = 1536
    o_ql, o_kv, o_kpe = 1552, 1808, 1936
    o_gq, o_gk, o_gv = 1968, 2224, 2352
    idx = np.full((IN_COLS_PADDED,), -1, np.int64)
    idx[0:1536] = np.arange(1536)
    idx[COL_MLA:COL_MLA + 256] = o_ql + np.arange(256)
    idx[COL_MLA + 256:COL_MLA + 384] = o_kv + np.arange(128)
    idx[COL_MLA + 384 + MLA_NOPE:COL_MLA + 384 + MLA_NOPE + MLA_ROPE] = o_kpe + np.arange(MLA_ROPE)
    for h in range(GQA_HEADS):
        idx[COL_GQA + h * LANES:COL_GQA + h * LANES + 64] = o_gq + h * 64 + np.arange(64)
    for j in range(GQA_KV_HEADS):
        for r in range(2):
            c0 = COL_GQA + 512 + j * LANES + r * 64
            idx[c0:c0 + 64] = o_gk + j * 64 + np.arange(64)
            idx[c0 + 256:c0 + 256 + 64] = o_gv + j * 64 + np.arange(64)
    for g in range(SSD_GROUPS):
        for d in range(2):
            c0 = COL_DT + g * LANES + d * HEADS_PER_GROUP
            idx[c0:c0 + HEADS_PER_GROUP] = o_dt + d * SSD_HEADS + g * HEADS_PER_GROUP + np.arange(HEADS_PER_GROUP)
    return idx


def _gather_cols(w, idx):
    cols = jnp.take(w, jnp.asarray(np.maximum(idx, 0)), axis=1)
    return jnp.where(jnp.asarray(idx >= 0)[None, :], cols, 0.0)


def _mla_weight_columns():
    per_q = MLA_NOPE + MLA_ROPE
    uq = np.full((MLA_HEADS * LANES,), -1, np.int64)
    uk = np.full((MLA_HEADS * LANES,), -1, np.int64)
    uv = np.full((MLA_HEADS * LANES,), -1, np.int64)
    for h in range(MLA_HEADS):
        uq[h * LANES:h * LANES + per_q] = h * per_q + np.arange(per_q)
        uk[h * LANES:h * LANES + MLA_NOPE] = h * (MLA_NOPE + MLA_V) + np.arange(MLA_NOPE)
        uv[h * LANES:h * LANES + MLA_V] = h * (MLA_NOPE + MLA_V) + MLA_NOPE + np.arange(MLA_V)
    return uq, uk, uv


def _rope_tables(n_tokens):
    t = np.arange(n_tokens)
    row, colp = (t // GRID_W).astype(np.float64), (t % GRID_W).astype(np.float64)

    def table(width, lane0, head_dim, reps):
        cos = np.ones((n_tokens, width))
        sin = np.zeros((n_tokens, width))
        half = head_dim // 2
        quarter = half // 2
        for dim in range(head_dim):
            pos = row if dim < half else colp
            inv = ROPE_BASE ** (-(2.0 * (dim % quarter)) / half)
            ang = pos * inv
            sign = -1.0 if (dim % half) < quarter else 1.0
            for r in range(reps):
                cos[:, lane0 + r * head_dim + dim] = np.cos(ang)
                sin[:, lane0 + r * head_dim + dim] = sign * np.sin(ang)
        return jnp.asarray(cos, F32), jnp.asarray(sin, F32)

    mcos, msin = table(LANES, MLA_NOPE, MLA_ROPE, 1)
    gcos, gsin = table(LANES, 0, GQA_HEAD_DIM, 2)
    return mcos, msin, gcos, gsin


def _trunk_layer(i, x, mod_i, mod_row, p, tables, ctx):
    b, l, d = x.shape
    tm = min(512, l)
    emit = ctx is None
    proj, dt = _in_projection(x, mod_i, mod_row, p["g_mix_pre"][i], p["w_in"][i], tm)
    h0 = None if ctx is None else ctx["state"][:, i]
    yssd, state = _ssd_mixer(proj, dt, p["conv_w"][i], p["conv_b"][i], p["ssd_par"][i], h0, emit)
    ctx_kv = None if ctx is None else ctx["kv"]
    ops = _attention_operands(proj, p["attn_w"][i], tables, ctx_kv, i, emit, tm)
    qm, km, vm, qg, kg, vg = ops[:6]
    tq = min(ATTN_TILE, l)
    nb = math.gcd(b, max(1, 1024 // km.shape[1]))
    ym = _attention(qm, km, vm, False, tq, nb)
    yg = _attention(qg, kg, vg, True, tq, nb)
    moe = i % 2 == 1
    j = i // 2
    x1, h2, comb, combt = _post_attention(
        yssd, proj, ym, yg, x, mod_i, mod_row, p["ssd_norm_g"][i], p["g_mix_post"][i], p["g_ffn_pre"][i],
        p["w_out"][i], p["w_router"][j] if moe else None, tm)
    if moe:
        sup = min(MOE_SUPER, b * l)
        f = _moe_ffn(h2.reshape(b * l, d), comb.reshape(b * l, LANES), combt,
                     p["moe_wg"][j], p["moe_wu"][j], p["moe_wd"][j], min(MOE_BLOCK, sup), sup)
        x2 = _gated_residual(f.reshape(b, l, d), x1, mod_i, mod_row, p["g_ffn_post"][i], tm)
    else:
        x2 = _dense_ffn(h2, x1, mod_i, mod_row, p["g_ffn_post"][i],
                        p["ffn_wg"][j], p["ffn_wu"][j], p["ffn_wd"][j], tm)
    return x2, (state,) + tuple(ops[6:])


def kernel(x_prompt, x_sample, state_ssd, cache_mla_ckv, cache_mla_krope, cache_gqa_k, cache_gqa_v, c, c_ctx, w_mod, b_mod, g_mix_pre, g_mix_post, g_ffn_pre, g_ffn_post, w_in, ssd_conv_w, ssd_conv_b, ssd_A_log, ssd_dt_bias, ssd_D, ssd_norm_g, mla_q_norm_g, mla_w_uq, mla_kv_norm_g, mla_w_ukv, gqa_q_norm_g, gqa_k_norm_g, w_out, ffn_w_gate, ffn_w_up, ffn_w_down, moe_w_router, moe_w_gate, moe_w_up, moe_w_down):
    depth = w_in.shape[0]
    d = x_prompt.shape[-1]
    n_dec = x_sample.shape[0]
    hpg = HEADS_PER_GROUP

    in_idx = _in_proj_columns()
    uq_idx, uk_idx, uv_idx = _mla_weight_columns()
    row2 = lambda a: a.reshape(depth, 1, a.shape[-1])
    def ssd_rows(fwd, bwd):
        cols = [fwd.reshape(depth, SSD_GROUPS, hpg), bwd.reshape(depth, SSD_GROUPS, hpg),
                jnp.zeros((depth, SSD_GROUPS, SSD_ROWS - 2 * hpg), F32)]
        return jnp.concatenate(cols, axis=-1)

    par = jnp.concatenate([ssd_rows(ssd_dt_bias[:, 0], ssd_dt_bias[:, 1]),
                           ssd_rows(ssd_A_log[:, 0], ssd_A_log[:, 1]),
                           ssd_rows(ssd_D, jnp.zeros_like(ssd_D))], axis=-1)
    par = jnp.broadcast_to(par[..., None], par.shape + (LANES,))
    zeros64 = jnp.zeros((depth, 1, GQA_HEAD_DIM), F32)
    attn_w = []
    for i in range(depth):
        attn_w.append((
            mla_q_norm_g[i][None, :], mla_kv_norm_g[i][None, :],
            _gather_cols(mla_w_uq[i], uq_idx).astype(BF16),
            _gather_cols(mla_w_ukv[i], uk_idx).astype(BF16),
            _gather_cols(mla_w_ukv[i], uv_idx).astype(BF16),
            jnp.concatenate([gqa_q_norm_g[i][None, :], zeros64[i]], axis=1),
            jnp.concatenate([gqa_k_norm_g[i][None, :]] * 2, axis=1),
        ))
    p = dict(
        g_mix_pre=row2(g_mix_pre), g_mix_post=row2(g_mix_post), g_ffn_pre=row2(g_ffn_pre),
        g_ffn_post=row2(g_ffn_post), ssd_norm_g=row2(ssd_norm_g),
        w_in=[_gather_cols(w_in[i], in_idx).astype(BF16) for i in range(depth)],
        conv_w=ssd_conv_w, conv_b=row2(ssd_conv_b), ssd_par=par, attn_w=attn_w,
        w_out=w_out.astype(BF16),
        ffn_wg=ffn_w_gate.astype(BF16), ffn_wu=ffn_w_up.astype(BF16), ffn_wd=ffn_w_down.astype(BF16),
        w_router=jnp.pad(moe_w_router, ((0, 0), (0, 0), (0, LANES - N_EXPERTS))),
        moe_wg=moe_w_gate.astype(BF16), moe_wu=moe_w_up.astype(BF16), moe_wd=moe_w_down.astype(BF16),
    )

    rows = -(-(1 + n_dec) // 8) * 8
    c_all = jnp.concatenate([c_ctx[None, :], c, jnp.zeros((rows - 1 - n_dec, d), F32)], axis=0)
    mod = _modulation(c_all, w_mod, b_mod).reshape(depth, rows, 1, 6 * d)

    y = x_prompt
    collected = [[], [], [], [], []]
    for i in range(depth):
        y, outs = _trunk_layer(i, y, mod[i], lambda bi: 0, p, None, None)
        for lst, t in zip(collected, outs):
            lst.append(t)
    y_prompt = y
    bsz, seq = x_prompt.shape[:2]
    new_state = jnp.stack(collected[0], axis=1)
    new_ckv = jnp.stack(collected[1], axis=1)
    new_krope = jnp.stack(collected[2], axis=1)
    new_k = jnp.stack(collected[3], axis=1).reshape(bsz, depth, seq, GQA_KV_HEADS, GQA_HEAD_DIM)
    new_v = jnp.stack(collected[4], axis=1).reshape(bsz, depth, seq, GQA_KV_HEADS, GQA_HEAD_DIM)

    past = cache_mla_ckv.shape[2]
    krope_blk = jnp.pad(cache_mla_krope, ((0, 0), (0, 0), (0, 0), (MLA_NOPE, LANES - MLA_NOPE - MLA_ROPE)))
    ctx = dict(
        state=state_ssd,
        kv=(cache_mla_ckv, krope_blk,
            cache_gqa_k.reshape(n_dec, depth, past, LANES), cache_gqa_v.reshape(n_dec, depth, past, LANES)),
    )
    tables = _rope_tables(x_sample.shape[1])
    y = x_sample
    for i in range(depth):
        y, _ = _trunk_layer(i, y, mod[i], lambda bi: bi + 1, p, tables, ctx)
    return (y_prompt, y, new_state, new_ckv, new_krope, new_k, new_v)
```

```python
import functools
import itertools
import math

import numpy as np
import jax
import jax.numpy as jnp
from jax import lax
from jax.experimental import pallas as pl
from jax.experimental.pallas import tpu as pltpu

F32 = jnp.float32
BF16 = jnp.bfloat16

EPS = 1e-6
ROPE_BASE = 10000.0
GRID_W = 64
SSD_HEAD_DIM = 64
SSD_HEADS = 8
SSD_GROUPS = 2
SSD_STATE = 128
SSD_CHUNK = 128
SSD_INNER = SSD_HEADS * SSD_HEAD_DIM
HEADS_PER_GROUP = SSD_HEADS // SSD_GROUPS
SSD_ROWS = 16
MLA_HEADS = 4
MLA_V = 64
MLA_NOPE = 64
MLA_ROPE = 32
MLA_Q_LORA = 256
MLA_KV_LORA = 128
GQA_HEADS = 4
GQA_KV_HEADS = 2
GQA_HEAD_DIM = 64
N_EXPERTS = 8

LANES = 128
BF16_SUBLANES = 16

COL_Z = 0
COL_X = 512
COL_B = 1024
COL_C = 1280
COL_MLA = 1536
COL_GQA = 2048
COL_DT = 3072
IN_COLS_PADDED = 3328

VMEM_LIMIT = 56 * 1024 * 1024

NT_DIMS = (((1,), (1,)), ((), ()))
TN_DIMS = (((0,), (0,)), ((), ()))


def _cparams(n_grid):
    return pltpu.CompilerParams(
        dimension_semantics=("arbitrary",) * n_grid, vmem_limit_bytes=VMEM_LIMIT)


def _const_spec(shape):
    nd = len(shape)
    return pl.BlockSpec(shape, lambda *_: (0,) * nd, pipeline_mode=pl.Buffered(1))


def _dot(a, b):
    return jnp.dot(a, b, preferred_element_type=F32)


def _rms(x, width=None):
    n = x.shape[-1] if width is None else width
    return x * lax.rsqrt(jnp.sum(x * x, axis=-1, keepdims=True) * (1.0 / n) + EPS)


def _silu(x):
    return x * jax.nn.sigmoid(x)


def _split3(v):
    hi = v.astype(BF16)
    r = v - hi.astype(F32)
    mid = r.astype(BF16)
    lo = (r - mid.astype(F32)).astype(BF16)
    return hi, mid, lo


def _mod_kernel(c_ref, w_ref, b_ref, o_ref):
    s = _silu(c_ref[...]).astype(BF16)
    o_ref[0] = _dot(s, w_ref[0].astype(BF16)) + b_ref[0]


def _modulation(c_all, w_mod, b_mod):
    depth, d, n = w_mod.shape
    tn = 1536
    rows = c_all.shape[0]
    return pl.pallas_call(
        _mod_kernel,
        grid=(depth, n // tn),
        in_specs=[
            pl.BlockSpec((rows, d), lambda i, j: (0, 0)),
            pl.BlockSpec((1, d, tn), lambda i, j: (i, 0, j)),
            pl.BlockSpec((1, 1, tn), lambda i, j: (i, 0, j)),
        ],
        out_specs=pl.BlockSpec((1, rows, tn), lambda i, j: (i, 0, j)),
        out_shape=jax.ShapeDtypeStruct((depth, rows, n), F32),
        compiler_params=_cparams(2),
        name="modulation",
    )(c_all, w_mod, b_mod.reshape(depth, 1, n))


def _inproj_kernel(x_ref, sh_ref, sc_ref, g_ref, w_ref, o_ref, dt_ref):
    h = _rms(x_ref[0]) * g_ref[...]
    h = h * (1.0 + sc_ref[0]) + sh_ref[0]
    proj = _dot(h.astype(BF16), w_ref[...])
    o_ref[0] = proj[:, 0:COL_DT].astype(BF16)
    dt_ref[0] = proj[:, COL_DT:]


def _in_projection(x, mod, mod_row, gain, w_in_p, tm):
    b, l, d = x.shape
    n = w_in_p.shape[1]
    n_dt = n - COL_DT
    return pl.pallas_call(
        _inproj_kernel,
        grid=(b, l // tm),
        in_specs=[
            pl.BlockSpec((1, tm, d), lambda i, t: (i, t, 0)),
            pl.BlockSpec((1, 1, d), lambda i, t: (mod_row(i), 0, 0)),
            pl.BlockSpec((1, 1, d), lambda i, t: (mod_row(i), 0, 1)),
            _const_spec((1, d)),
            _const_spec((d, n)),
        ],
        out_specs=[pl.BlockSpec((1, tm, COL_DT), lambda i, t: (i, t, 0)),
                   pl.BlockSpec((1, tm, n_dt), lambda i, t: (i, t, 0))],
        out_shape=[jax.ShapeDtypeStruct((b, l, COL_DT), BF16), jax.ShapeDtypeStruct((b, l, n_dt), F32)],
        compiler_params=_cparams(2),
        name="in_projection",
    )(x, mod, mod, gain, w_in_p)


def _ssd_kernel(*refs, nc, seq, has_h0, emit_state):
    it = iter(refs)
    x_ref, b_ref, c_ref, dt_ref = next(it), next(it), next(it), next(it)
    cwx_ref, cwb_ref, cwc_ref = next(it), next(it), next(it)
    cbx_ref, cbb_ref, cbc_ref = next(it), next(it), next(it)
    par_ref = next(it)
    h0_ref = next(it) if has_h0 else None
    y_ref = next(it)
    st_ref = next(it) if emit_state else None
    xt_s, yd_s, bc_s, dtv_s, arow_s, acol_s, h_s = (next(it) for _ in range(7))

    q = SSD_CHUNK
    hpg = HEADS_PER_GROUP
    hd = SSD_HEAD_DIM
    wx = hpg * hd
    r = SSD_ROWS
    pack = BF16_SUBLANES
    row = lax.broadcasted_iota(jnp.int32, (q, 1), 0)
    si = lax.broadcasted_iota(jnp.int32, (q, q), 0)
    ti = lax.broadcasted_iota(jnp.int32, (q, q), 1)
    le = si <= ti
    ge = si >= ti
    tri_le = jnp.where(le, 1.0, 0.0).astype(BF16)
    tri_ge = jnp.where(ge, 1.0, 0.0).astype(BF16)
    par = par_ref[0]
    bias_t, alog_t, d_t = par[0:r], par[r:2 * r], par[2 * r:3 * r]

    for d in range(2):
        if has_h0:
            h_s[d] = h0_ref[0, d].reshape(wx, SSD_STATE)
        else:
            h_s[d] = jnp.zeros((wx, SSD_STATE), F32)

    def conv_silu(ref, w_ref, bias_ref, c):
        off = pl.multiple_of(c * q, q)
        u = ref[0, pl.ds(off, q), :].astype(F32)
        before = ref[0, pl.ds(pl.multiple_of(jnp.maximum(off - pack, 0), pack), pack), :]
        after = ref[0, pl.ds(pl.multiple_of(jnp.minimum(off + q, seq - pack), pack), pack), :]
        prev = before[pack - 1:pack, :].astype(F32) * jnp.where(c > 0, 1.0, 0.0)
        nxt = after[0:1, :].astype(F32) * jnp.where(c < nc - 1, 1.0, 0.0)
        up = jnp.where(row == 0, prev, pltpu.roll(u, 1, 0))
        un = jnp.where(row == q - 1, nxt, pltpu.roll(u, q - 1, 0))
        w = w_ref[...]
        return _silu(up * w[0:1] + u * w[1:2] + un * w[2:3] + bias_ref[...])

    tri_rows = jnp.concatenate([tri_le, tri_ge], axis=1)
    tri_cols = jnp.concatenate([tri_ge, tri_le], axis=0)
    fwd_row = lax.broadcasted_iota(jnp.int32, (r, 1), 0) < hpg
    fwd_lane = lax.broadcasted_iota(jnp.int32, (1, r), 1) < hpg

    def prep_body(c, carry):
        off = pl.multiple_of(c * q, q)
        xt = conv_silu(x_ref, cwx_ref, cbx_ref, c).T
        xt_s[c] = xt
        bc_s[pl.ds(off, q), 0:SSD_STATE] = conv_silu(b_ref, cwb_ref, cbb_ref, c).astype(BF16)
        bc_s[pl.ds(off, q), SSD_STATE:2 * SSD_STATE] = conv_silu(c_ref, cwc_ref, cbc_ref, c).astype(BF16)
        dtr = dt_ref[0, pl.ds(off, q), :].T[0:r, :] + bias_t
        dtv = jnp.maximum(dtr, 0.0) + jnp.log1p(jnp.exp(-jnp.abs(dtr)))
        dtv_s[c] = dtv
        parts = _split3(dtv * (-jnp.exp(alog_t)))
        rows = sum(_dot(part, tri_rows) for part in parts)
        cols = sum(lax.dot_general(tri_cols, part, NT_DIMS, preferred_element_type=F32) for part in parts)
        arow_s[c] = jnp.where(fwd_row, rows[:, 0:q], rows[:, q:2 * q])
        acol_s[c] = jnp.where(fwd_lane, cols[0:q, :], cols[q:2 * q, :])
        return carry

    def chunk(c, d):
        off = pl.multiple_of(c * q, q)
        base, mask = (0, le) if d == 0 else (hpg, ge)
        xt = xt_s[c]
        b16 = bc_s[pl.ds(off, q), 0:SSD_STATE]
        c16 = bc_s[pl.ds(off, q), SSD_STATE:2 * SSD_STATE]
        dtv, a_rows, a_cols = dtv_s[c], arow_s[c], acol_s[c]
        tot = a_rows[:, q - 1:q] if d == 0 else a_rows[:, 0:1]
        st = lax.dot_general(b16, c16, NT_DIMS, preferred_element_type=F32)
        hs = h_s[d]
        yoff = lax.dot_general(hs.astype(BF16), c16, NT_DIMS, preferred_element_type=F32)
        yield
        outs, xdecs, hnew = [], [], []
        for hl in range(hpg):
            k = base + hl
            rs = slice(hl * hd, (hl + 1) * hd)
            a_t = a_rows[k:k + 1, :]
            tot_k = tot[k:k + 1, :]
            seg = a_t - a_cols[:, k:k + 1]
            w = (st * jnp.exp(jnp.where(mask, seg, -jnp.inf))).astype(BF16)
            xdt = xt[rs, :] * dtv[k:k + 1, :]
            outs.append(_dot(xdt.astype(BF16), w) + yoff[rs, :] * jnp.exp(a_t))
            xdecs.append((xdt * jnp.exp(tot_k - a_t)).astype(BF16))
            hnew.append(hs[rs, :] * jnp.exp(tot_k))
            yield
        h_s[d] = jnp.concatenate(hnew, axis=0) + _dot(jnp.concatenate(xdecs, axis=0), b16)
        yd_s[d, c] = jnp.concatenate(outs, axis=0)

    def scan_body(i, carry):
        for _ in itertools.zip_longest(chunk(i, 0), chunk(nc - 1 - i, 1)):
            pass
        return carry

    def out_body(c, carry):
        xt = xt_s[c]
        skip = [xt[hl * hd:(hl + 1) * hd, :] * d_t[hl:hl + 1, :] for hl in range(hpg)]
        yt = yd_s[0, c] + yd_s[1, c] + jnp.concatenate(skip, axis=0)
        y_ref[0, pl.ds(pl.multiple_of(c * q, q), q), :] = yt.T
        return carry

    lax.fori_loop(0, nc, prep_body, 0, unroll=4)
    lax.fori_loop(0, nc, scan_body, 0, unroll=2)
    lax.fori_loop(0, nc, out_body, 0, unroll=4)
    if emit_state:
        for d in range(2):
            st_ref[0, d] = h_s[d].reshape(hpg, SSD_HEAD_DIM, SSD_STATE)


def _ssd_mixer(proj, dt, conv_w, conv_b, par, h0, emit_state):
    b, l, _ = proj.shape
    g = SSD_GROUPS
    wx = HEADS_PER_GROUP * SSD_HEAD_DIM
    n = SSD_STATE
    nc = l // SSD_CHUNK
    has_h0 = h0 is not None

    def col(width, start):
        blk = start // width
        return pl.BlockSpec((1, l, width), lambda i, j: (i, 0, blk + j))

    def cw(width, start, rows):
        blk = start // width
        return pl.BlockSpec((rows, width), lambda i, j: (0, blk + j))

    in_specs = [
        col(wx, COL_X), col(n, COL_B), col(n, COL_C), col(LANES, 0),
        cw(wx, 0, 3), cw(n, SSD_INNER, 3), cw(n, SSD_INNER + g * n, 3),
        cw(wx, 0, 1), cw(n, SSD_INNER, 1), cw(n, SSD_INNER + g * n, 1),
        pl.BlockSpec((1, 3 * SSD_ROWS, LANES), lambda i, j: (j, 0, 0)),
    ]
    args = [proj, proj, proj, dt, conv_w, conv_w, conv_w, conv_b, conv_b, conv_b, par]
    state_spec = pl.BlockSpec((1, 2, HEADS_PER_GROUP, SSD_HEAD_DIM, n), lambda i, j: (i, 0, j, 0, 0))
    if has_h0:
        in_specs.append(state_spec)
        args.append(h0)
    out_specs = [pl.BlockSpec((1, l, wx), lambda i, j: (i, 0, j))]
    out_shape = [jax.ShapeDtypeStruct((b, l, SSD_INNER), F32)]
    if emit_state:
        out_specs.append(state_spec)
        out_shape.append(jax.ShapeDtypeStruct((b, 2, SSD_HEADS, SSD_HEAD_DIM, n), F32))
    outs = pl.pallas_call(
        functools.partial(_ssd_kernel, nc=nc, seq=l, has_h0=has_h0, emit_state=emit_state),
        grid=(b, g),
        in_specs=in_specs,
        out_specs=out_specs,
        out_shape=out_shape,
        scratch_shapes=[
            pltpu.VMEM((nc, wx, SSD_CHUNK), F32),
            pltpu.VMEM((2, nc, wx, SSD_CHUNK), F32),
            pltpu.VMEM((l, 2 * n), BF16),
            pltpu.VMEM((nc, SSD_ROWS, SSD_CHUNK), F32),
            pltpu.VMEM((nc, SSD_ROWS, SSD_CHUNK), F32),
            pltpu.VMEM((nc, SSD_CHUNK, SSD_ROWS), F32),
            pltpu.VMEM((2, wx, n), F32),
        ],
        compiler_params=_cparams(2),
        name="ssd_mixer",
    )(*args)
    return (outs[0], outs[1]) if emit_state else (outs[0], None)


def _rope(x, cos, sin_signed, pair):
    n = x.shape[1]
    lane = lax.broadcasted_iota(jnp.int32, (1, n), 1)
    first = (lane & (2 * pair - 1)) < pair
    partner = jnp.where(first, pltpu.roll(x, n - pair, 1), pltpu.roll(x, pair, 1))
    return x * cos + partner * sin_signed


def _tile_lanes(x, k):
    return jnp.concatenate([x] * k, axis=1)


def _qkv_kernel(*refs, nct, has_pos, emit_cache):
    it = iter(refs)
    mla_ref, gqa_ref = next(it), next(it)
    qg_ref, kvg_ref, wuq_ref, wk_ref, wv_ref, gq_ref, gk_ref = (next(it) for _ in range(7))
    if has_pos:
        mcos_ref, msin_ref, gcos_ref, gsin_ref = (next(it) for _ in range(4))
    if nct:
        cckv_ref, ckr_ref, cgk_ref, cgv_ref = (next(it) for _ in range(4))
    qm_ref, km_ref, vm_ref, qg_out, kg_ref, vg_ref = (next(it) for _ in range(6))
    if emit_cache:
        ckv_out, kpe_out, kn_out, v_out = (next(it) for _ in range(4))

    t = pl.program_id(1)
    lane = lax.broadcasted_iota(jnp.int32, (1, LANES), 1)
    first_half = lane < GQA_HEAD_DIM
    mla_scale = math.log2(math.e) / math.sqrt(MLA_NOPE + MLA_ROPE)
    gqa_scale = math.log2(math.e) / math.sqrt(GQA_HEAD_DIM)
    ones_lane = jnp.where(lane == GQA_HEAD_DIM, 1.0, 0.0)

    def write_mla_kv(ckv, kpe_r):
        c16 = ckv.astype(BF16)
        km_ref[0] = (_dot(c16, wk_ref[...]) + _tile_lanes(kpe_r, MLA_HEADS)).astype(BF16)
        vm_ref[0] = (_dot(c16, wv_ref[...]) + _tile_lanes(ones_lane, MLA_HEADS)).T.astype(BF16)

    def new_tile():
        m = mla_ref[0].astype(F32)
        q_lat = m[:, 0:MLA_Q_LORA]
        kv_lat = m[:, MLA_Q_LORA:MLA_Q_LORA + MLA_KV_LORA]
        kpe = m[:, MLA_Q_LORA + MLA_KV_LORA:]
        qn = _rms(q_lat) * qg_ref[...]
        qm = _dot(qn.astype(BF16), wuq_ref[...])
        ckv = _rms(kv_lat) * kvg_ref[...]
        kpe_r = kpe
        if has_pos:
            mcos, msin = mcos_ref[...], msin_ref[...]
            qm = _rope(qm, _tile_lanes(mcos, MLA_HEADS), _tile_lanes(msin, MLA_HEADS), MLA_ROPE // 4)
            kpe_r = _rope(kpe, mcos, msin, MLA_ROPE // 4)
        qm_ref[0] = (qm * mla_scale).astype(BF16)
        write_mla_kv(ckv, kpe_r)

        g = gqa_ref[0].astype(F32)
        if has_pos:
            gcos, gsin = gcos_ref[...], gsin_ref[...]
        for h in range(GQA_HEADS):
            xh = g[:, h * LANES:(h + 1) * LANES]
            qh = _rms(xh, GQA_HEAD_DIM) * gq_ref[...]
            if has_pos:
                qh = _rope(qh, gcos, gsin, GQA_HEAD_DIM // 4)
            qg_out[0, :, h * LANES:(h + 1) * LANES] = (qh * gqa_scale).astype(BF16)
        kns = []
        for j in range(GQA_KV_HEADS):
            c0 = GQA_HEADS * LANES + j * LANES
            kn = _rms(g[:, c0:c0 + LANES]) * gk_ref[...]
            kns.append(kn)
            kr = _rope(kn, gcos, gsin, GQA_HEAD_DIM // 4) if has_pos else kn
            kg_ref[0, :, j * LANES:(j + 1) * LANES] = kr.astype(BF16)
        v0 = (GQA_HEADS + GQA_KV_HEADS) * LANES
        v = g[:, v0:v0 + GQA_KV_HEADS * LANES]
        for j in range(GQA_KV_HEADS):
            vj = jnp.where(first_half, v[:, j * LANES:(j + 1) * LANES], ones_lane)
            vg_ref[0, j * LANES:(j + 1) * LANES, :] = vj.T.astype(BF16)
        if emit_cache:
            ckv_out[0] = ckv
            kpe_out[0] = kpe[:, MLA_NOPE:MLA_NOPE + MLA_ROPE]
            kn_out[0] = jnp.where(first_half, kns[0], kns[1])
            v_out[0] = jnp.where(first_half, v[:, 0:LANES], v[:, LANES:2 * LANES])

    def ctx_tile():
        write_mla_kv(cckv_ref[0], ckr_ref[0])
        k = cgk_ref[0]
        k_sw = pltpu.roll(k, GQA_HEAD_DIM, 1)
        kg_ref[0, :, 0:LANES] = jnp.where(first_half, k, k_sw).astype(BF16)
        kg_ref[0, :, LANES:2 * LANES] = jnp.where(first_half, k_sw, k).astype(BF16)
        v = cgv_ref[0]
        vg_ref[0, 0:LANES, :] = jnp.where(first_half, v, ones_lane).T.astype(BF16)
        vg_ref[0, LANES:2 * LANES, :] = jnp.where(first_half, pltpu.roll(v, GQA_HEAD_DIM, 1), ones_lane).T.astype(BF16)

    if nct:
        pl.when(t < nct)(ctx_tile)
        pl.when(t >= nct)(new_tile)
    else:
        new_tile()


def _attention_operands(proj, wts, tables, ctx, layer, emit_cache, tm):
    b, l, _ = proj.shape
    has_pos = tables is not None
    nct = 0 if ctx is None else ctx[0].shape[2] // tm
    lk = l + nct * tm
    nt = lk // tm

    def new_t(t):
        return jnp.maximum(t - nct, 0) if nct else t

    in_specs = [
        pl.BlockSpec((1, tm, 512), lambda i, t: (i, new_t(t), COL_MLA // 512)),
        pl.BlockSpec((1, tm, 1024), lambda i, t: (i, new_t(t), COL_GQA // 1024)),
    ] + [_const_spec(w.shape) for w in wts]
    args = [proj, proj] + list(wts)
    if has_pos:
        in_specs += [pl.BlockSpec((tm, LANES), lambda i, t: (new_t(t), 0))] * 4
        args += list(tables)
    if nct:
        in_specs += [pl.BlockSpec((1, None, tm, LANES), lambda i, t: (i, layer, jnp.minimum(t, nct - 1), 0))] * 4
        args += list(ctx)
    q_spec = lambda w: pl.BlockSpec((1, tm, w), lambda i, t: (i, new_t(t), 0))
    k_spec = lambda w: pl.BlockSpec((1, tm, w), lambda i, t: (i, t, 0))
    vt_spec = lambda w: pl.BlockSpec((1, w, tm), lambda i, t: (i, 0, t))
    out_specs = [q_spec(512), k_spec(512), vt_spec(512), q_spec(512), k_spec(256), vt_spec(256)]
    out_shape = [
        jax.ShapeDtypeStruct((b, l, 512), BF16), jax.ShapeDtypeStruct((b, lk, 512), BF16),
        jax.ShapeDtypeStruct((b, 512, lk), BF16), jax.ShapeDtypeStruct((b, l, 512), BF16),
        jax.ShapeDtypeStruct((b, lk, 256), BF16), jax.ShapeDtypeStruct((b, 256, lk), BF16),
    ]
    if emit_cache:
        out_specs += [q_spec(MLA_KV_LORA), q_spec(MLA_ROPE), q_spec(LANES), q_spec(LANES)]
        out_shape += [
            jax.ShapeDtypeStruct((b, l, MLA_KV_LORA), F32), jax.ShapeDtypeStruct((b, l, MLA_ROPE), F32),
            jax.ShapeDtypeStruct((b, l, LANES), F32), jax.ShapeDtypeStruct((b, l, LANES), F32),
        ]
    return pl.pallas_call(
        functools.partial(_qkv_kernel, nct=nct, has_pos=has_pos, emit_cache=emit_cache),
        grid=(b, nt),
        in_specs=in_specs,
        out_specs=out_specs,
        out_shape=out_shape,
        compiler_params=_cparams(2),
        name="attention_operands",
    )(*args)


def _attn_kernel(q_ref, k_ref, vt_ref, o_ref, *, shared_kv):
    lane = lax.broadcasted_iota(jnp.int32, (1, LANES), 1)
    heads = (0, 1)
    cols = [slice(0, LANES) if shared_kv else slice(a * LANES, (a + 1) * LANES) for a in heads]
    tsub = min(ATTN_SUBTILE, q_ref.shape[1])
    units = [(i, r) for i in range(q_ref.shape[0]) for r in range(0, q_ref.shape[1], tsub)]

    def scores(u):
        i, r = u
        return [lax.dot_general(k_ref[i, :, cols[a]], q_ref[i, r:r + tsub, a * LANES:(a + 1) * LANES], NT_DIMS,
                                preferred_element_type=F32) for a in heads]

    def finish(u, sts):
        i, r = u
        pts = []
        for st in sts:
            part = jnp.max(st.reshape(-1, 2 * LANES, st.shape[1]), axis=0)
            pts.append(jnp.exp2(st - jnp.max(part, axis=0, keepdims=True)).astype(BF16))
        accs = [_dot(vt_ref[i, cols[a], :], pts[a]) for a in heads]
        outs = [(acc / acc[GQA_HEAD_DIM:GQA_HEAD_DIM + 1, :]).T for acc in accs]
        o_ref[i, r:r + tsub, :] = jnp.where(
            lane < GQA_HEAD_DIM, outs[0], pltpu.roll(outs[1], GQA_HEAD_DIM, 1)).astype(BF16)

    sts = scores(units[0])
    for n, u in enumerate(units):
        nxt = scores(units[n + 1]) if n + 1 < len(units) else None
        finish(u, sts)
        sts = nxt


def _attention(q, k, v, shared_kv, tq, nb):
    b, l, _ = q.shape
    lk = k.shape[1]
    kw = LANES if shared_kv else 2 * LANES
    return pl.pallas_call(
        functools.partial(_attn_kernel, shared_kv=shared_kv),
        grid=(b // nb, 2, l // tq),
        in_specs=[
            pl.BlockSpec((nb, tq, 2 * LANES), lambda i, j, t: (i, t, j)),
            pl.BlockSpec((nb, lk, kw), lambda i, j, t: (i, 0, j)),
            pl.BlockSpec((nb, kw, lk), lambda i, j, t: (i, j, 0)),
        ],
        out_specs=pl.BlockSpec((nb, tq, LANES), lambda i, j, t: (i, t, j)),
        out_shape=jax.ShapeDtypeStruct((b, l, 2 * LANES), BF16),
        compiler_params=_cparams(3),
        name="attention_shared_kv" if shared_kv else "attention",
    )(q, k, v)


def _post_kernel(*refs, moe):
    it = iter(refs)
    ys_ref, z_ref, ym_ref, yg_ref, x_ref, g1_ref, sh2_ref, sc2_ref = (next(it) for _ in range(8))
    ng_ref, gpost_ref, gpre_ref, wout_ref = (next(it) for _ in range(4))
    wr_ref = next(it) if moe else None
    x1_ref, h2_ref = next(it), next(it)
    comb_ref, combt_ref = (next(it), next(it)) if moe else (None, None)

    tm = x_ref.shape[1]
    if moe:
        w_hi = wr_ref[...].astype(BF16)
        w_lo = (wr_ref[...] - w_hi.astype(F32)).astype(BF16)
        lane = lax.broadcasted_iota(jnp.int32, (1, LANES), 1).astype(F32)

    def rows(r0, r1):
        y = _rms(ys_ref[0, r0:r1, :] * _silu(z_ref[0, r0:r1, :].astype(F32))) * ng_ref[...]
        cat = jnp.concatenate([y.astype(BF16), ym_ref[0, r0:r1, :], yg_ref[0, r0:r1, :]], axis=1)
        yield
        mix = _dot(cat, wout_ref[...])
        yield
        x1 = x_ref[0, r0:r1, :] + g1_ref[0] * (_rms(mix) * gpost_ref[...])
        x1_ref[0, r0:r1, :] = x1
        h2 = _rms(x1) * gpre_ref[...] * (1.0 + sc2_ref[0]) + sh2_ref[0]
        h16 = h2.astype(BF16)
        h2_ref[0, r0:r1, :] = h16
        if moe:
            yield
            logits = _dot(h16, w_hi) + _dot(h16, w_lo)
            yield
            lg = jnp.where(lane < N_EXPERTS, logits, -jnp.inf)
            m1 = jnp.max(lg, axis=-1, keepdims=True)
            i1 = jnp.min(jnp.where(lg == m1, lane, float(LANES)), axis=-1, keepdims=True)
            lg2 = jnp.where(lane == i1, -jnp.inf, lg)
            m2 = jnp.max(lg2, axis=-1, keepdims=True)
            i2 = jnp.min(jnp.where(lg2 == m2, lane, float(LANES)), axis=-1, keepdims=True)
            e = jnp.exp(m2 - m1)
            comb = jnp.where(lane == i1, 1.0 / (1.0 + e), jnp.where(lane == i2, e / (1.0 + e), 0.0))
            comb_ref[0, r0:r1, :] = comb
            combt_ref[:, r0:r1] = comb.T[0:N_EXPERTS, :]

    half = tm // 2 if tm % 256 == 0 else tm
    pending = [rows(r, r + half) for r in range(0, tm, half)]
    live = []
    while pending or live:
        if pending:
            live.append(pending.pop(0))
        for g in list(live):
            if next(g, StopIteration) is StopIteration:
                live.remove(g)


def _post_attention(yssd, proj, ym, yg, x, mod, mod_row, ng, gpost, gpre, w_out, w_router, tm):
    b, l, d = x.shape
    moe = w_router is not None
    tok = lambda w: pl.BlockSpec((1, tm, w), lambda i, t: (i, t, 0))
    modk = lambda k: pl.BlockSpec((1, 1, d), lambda i, t: (mod_row(i), 0, k))
    in_specs = [tok(SSD_INNER), tok(SSD_INNER), tok(256), tok(256), tok(d), modk(2), modk(3), modk(4),
                _const_spec(ng.shape), _const_spec(gpost.shape), _const_spec(gpre.shape),
                _const_spec(w_out.shape)]
    args = [yssd, proj, ym, yg, x, mod, mod, mod, ng, gpost, gpre, w_out]
    out_specs = [tok(d), tok(d)]
    out_shape = [jax.ShapeDtypeStruct((b, l, d), F32), jax.ShapeDtypeStruct((b, l, d), BF16)]
    if moe:
        in_specs.append(_const_spec(w_router.shape))
        args.append(w_router)
        out_specs += [tok(LANES), pl.BlockSpec((N_EXPERTS, tm), lambda i, t: (0, i * (l // tm) + t))]
        out_shape += [jax.ShapeDtypeStruct((b, l, LANES), F32), jax.ShapeDtypeStruct((N_EXPERTS, b * l), F32)]
    outs = pl.pallas_call(
        functools.partial(_post_kernel, moe=moe),
        grid=(b, l // tm),
        in_specs=in_specs,
        out_specs=out_specs,
        out_shape=out_shape,
        compiler_params=_cparams(2),
        name="post_attention",
    )(*args)
    return outs if moe else (outs[0], outs[1], None, None)


def _ffn_kernel(h_ref, x1_ref, g2_ref, gp_ref, wg_ref, wu_ref, wd_ref, o_ref):
    h = h_ref[0]
    act = (_silu(_dot(h, wg_ref[...])) * _dot(h, wu_ref[...])).astype(BF16)
    f = _dot(act, wd_ref[...])
    o_ref[0] = x1_ref[0] + g2_ref[0] * (_rms(f) * gp_ref[...])


def _dense_ffn(h2, x1, mod, mod_row, gp, wg, wu, wd, tm):
    b, l, d = x1.shape
    tok = pl.BlockSpec((1, tm, d), lambda i, t: (i, t, 0))
    return pl.pallas_call(
        _ffn_kernel,
        grid=(b, l // tm),
        in_specs=[tok, tok, pl.BlockSpec((1, 1, d), lambda i, t: (mod_row(i), 0, 5)),
                  _const_spec(gp.shape), _const_spec(wg.shape), _const_spec(wu.shape), _const_spec(wd.shape)],
        out_specs=tok,
        out_shape=jax.ShapeDtypeStruct((b, l, d), F32),
        compiler_params=_cparams(2),
        name="dense_ffn",
    )(h2, x1, mod, gp, wg, wu, wd)


ATTN_TILE = 2048
ATTN_SUBTILE = 256

MOE_BLOCK = 1024
MOE_CAP = 256
MOE_SUPER = 2048


def _moe_kernel(h_ref, comb_ref, combt_ref, wg_ref, wu_ref, wd_ref, o_ref, pos_s, *, tb, cap):
    e = pl.program_id(1)
    nsub = o_ref.shape[0] // tb

    @pl.when(e == 0)
    def _():
        o_ref[...] = jnp.zeros(o_ref.shape, F32)
        si = lax.broadcasted_iota(jnp.int32, (tb, tb), 0)
        ti = lax.broadcasted_iota(jnp.int32, (tb, tb), 1)
        before = jnp.where(si < ti, 1.0, 0.0).astype(BF16)
        for u in range(nsub):
            sel = jnp.where(combt_ref[:, u * tb:(u + 1) * tb] > 0.0, 1.0, 0.0).astype(BF16)
            pos_s[:, u * tb:(u + 1) * tb] = _dot(sel, before)

    lane = lax.broadcasted_iota(jnp.int32, (1, LANES), 1)
    slot = lax.broadcasted_iota(jnp.int32, (cap, 1), 0).astype(F32)

    def block(u, carry):
        off = pl.multiple_of(u * tb, tb)
        sel_e = combt_ref[pl.ds(e, 1), pl.ds(off, tb)] > 0.0
        pos_e = pos_s[pl.ds(e, 1), pl.ds(off, tb)]
        n_e = jnp.sum(jnp.where(sel_e, 1.0, 0.0)).astype(jnp.int32)
        gate = jnp.sum(jnp.where(lane == e, comb_ref[pl.ds(off, tb), :], 0.0), axis=-1, keepdims=True)

        def tile(j, carry):
            first = j * float(cap)
            pick = jnp.where(sel_e & (pos_e - first == slot), 1.0, 0.0).astype(BF16)
            xg = _dot(pick, h_ref[pl.ds(off, tb), :]).astype(BF16)
            act = (_silu(_dot(xg, wg_ref[0])) * _dot(xg, wu_ref[0])).astype(BF16)
            y = _dot(act, wd_ref[0]).astype(BF16)
            o_ref[pl.ds(off, tb), :] += gate * lax.dot_general(pick, y, TN_DIMS, preferred_element_type=F32)
            return carry

        lax.fori_loop(0, (n_e + cap - 1) // cap, tile, 0)
        return carry

    lax.fori_loop(0, nsub, block, 0)


def _moe_ffn(h2, comb, combt, wg, wu, wd, tb, sup):
    t, d = h2.shape
    n_e, _, ff = wg.shape
    cap = -(-(MOE_CAP * tb // MOE_BLOCK) // 16) * 16
    once = pl.Buffered(1)
    return pl.pallas_call(
        functools.partial(_moe_kernel, tb=tb, cap=cap),
        grid=(t // sup, n_e),
        in_specs=[pl.BlockSpec((sup, d), lambda s, e: (s, 0), pipeline_mode=once),
                  pl.BlockSpec((sup, LANES), lambda s, e: (s, 0), pipeline_mode=once),
                  pl.BlockSpec((N_EXPERTS, sup), lambda s, e: (0, s), pipeline_mode=once),
                  pl.BlockSpec((1, d, ff), lambda s, e: (e, 0, 0)),
                  pl.BlockSpec((1, d, ff), lambda s, e: (e, 0, 0)),
                  pl.BlockSpec((1, ff, d), lambda s, e: (e, 0, 0))],
        out_specs=pl.BlockSpec((sup, d), lambda s, e: (s, 0), pipeline_mode=once),
        out_shape=jax.ShapeDtypeStruct((t, d), F32),
        scratch_shapes=[pltpu.VMEM((N_EXPERTS, sup), F32)],
        compiler_params=_cparams(2),
        name="moe_ffn",
    )(h2, comb, combt, wg, wu, wd)


def _residual_kernel(f_ref, x1_ref, g2_ref, gp_ref, o_ref):
    o_ref[0] = x1_ref[0] + g2_ref[0] * (_rms(f_ref[0]) * gp_ref[...])


def _gated_residual(f, x1, mod, mod_row, gp, tm):
    b, l, d = x1.shape
    tok = pl.BlockSpec((1, tm, d), lambda i, t: (i, t, 0))
    return pl.pallas_call(
        _residual_kernel,
        grid=(b, l // tm),
        in_specs=[tok, tok, pl.BlockSpec((1, 1, d), lambda i, t: (mod_row(i), 0, 5)), _const_spec(gp.shape)],
        out_specs=tok,
        out_shape=jax.ShapeDtypeStruct((b, l, d), F32),
        compiler_params=_cparams(2),
        name="gated_residual",
    )(f, x1, mod, gp)


def _in_proj_columns():
    o_dt = 1536
    o_ql, o_kv, o_kpe = 1552, 1808, 1936
    o_gq, o_gk, o_gv = 1968, 2224, 2352
    idx = np.full((IN_COLS_PADDED,), -1, np.int64)
    idx[0:1536] = np.arange(1536)
    idx[COL_MLA:COL_MLA + 256] = o_ql + np.arange(256)
    idx[COL_MLA + 256:COL_MLA + 384] = o_kv + np.arange(128)
    idx[COL_MLA + 384 + MLA_NOPE:COL_MLA + 384 + MLA_NOPE + MLA_ROPE] = o_kpe + np.arange(MLA_ROPE)
    for h in range(GQA_HEADS):
        idx[COL_GQA + h * LANES:COL_GQA + h * LANES + 64] = o_gq + h * 64 + np.arange(64)
    for j in range(GQA_KV_HEADS):
        for r in range(2):
            c0 = COL_GQA + 512 + j * LANES + r * 64
            idx[c0:c0 + 64] = o_gk + j * 64 + np.arange(64)
            idx[c0 + 256:c0 + 256 + 64] = o_gv + j * 64 + np.arange(64)
    for g in range(SSD_GROUPS):
        for d in range(2):
            c0 = COL_DT + g * LANES + d * HEADS_PER_GROUP
            idx[c0:c0 + HEADS_PER_GROUP] = o_dt + d * SSD_HEADS + g * HEADS_PER_GROUP + np.arange(HEADS_PER_GROUP)
    return idx


def _gather_cols(w, idx):
    cols = jnp.take(w, jnp.asarray(np.maximum(idx, 0)), axis=1)
    return jnp.where(jnp.asarray(idx >= 0)[None, :], cols, 0.0)


def _mla_weight_columns():
    per_q = MLA_NOPE + MLA_ROPE
    uq = np.full((MLA_HEADS * LANES,), -1, np.int64)
    uk = np.full((MLA_HEADS * LANES,), -1, np.int64)
    uv = np.full((MLA_HEADS * LANES,), -1, np.int64)
    for h in range(MLA_HEADS):
        uq[h * LANES:h * LANES + per_q] = h * per_q + np.arange(per_q)
        uk[h * LANES:h * LANES + MLA_NOPE] = h * (MLA_NOPE + MLA_V) + np.arange(MLA_NOPE)
        uv[h * LANES:h * LANES + MLA_V] = h * (MLA_NOPE + MLA_V) + MLA_NOPE + np.arange(MLA_V)
    return uq, uk, uv


def _rope_tables(n_tokens):
    t = np.arange(n_tokens)
    row, colp = (t // GRID_W).astype(np.float64), (t % GRID_W).astype(np.float64)

    def table(width, lane0, head_dim, reps):
        cos = np.ones((n_tokens, width))
        sin = np.zeros((n_tokens, width))
        half = head_dim // 2
        quarter = half // 2
        for dim in range(head_dim):
            pos = row if dim < half else colp
            inv = ROPE_BASE ** (-(2.0 * (dim % quarter)) / half)
            ang = pos * inv
            sign = -1.0 if (dim % half) < quarter else 1.0
            for r in range(reps):
                cos[:, lane0 + r * head_dim + dim] = np.cos(ang)
                sin[:, lane0 + r * head_dim + dim] = sign * np.sin(ang)
        return jnp.asarray(cos, F32), jnp.asarray(sin, F32)

    mcos, msin = table(LANES, MLA_NOPE, MLA_ROPE, 1)
    gcos, gsin = table(LANES, 0, GQA_HEAD_DIM, 2)
    return mcos, msin, gcos, gsin


def _trunk_layer(i, x, mod_i, mod_row, p, tables, ctx):
    b, l, d = x.shape
    tm = min(512, l)
    emit = ctx is None
    proj, dt = _in_projection(x, mod_i, mod_row, p["g_mix_pre"][i], p["w_in"][i], tm)
    h0 = None if ctx is None else ctx["state"][:, i]
    yssd, state = _ssd_mixer(proj, dt, p["conv_w"][i], p["conv_b"][i], p["ssd_par"][i], h0, emit)
    ctx_kv = None if ctx is None else ctx["kv"]
    ops = _attention_operands(proj, p["attn_w"][i], tables, ctx_kv, i, emit, tm)
    qm, km, vm, qg, kg, vg = ops[:6]
    tq = min(ATTN_TILE, l)
    nb = math.gcd(b, max(1, 1024 // km.shape[1]))
    ym = _attention(qm, km, vm, False, tq, nb)
    yg = _attention(qg, kg, vg, True, tq, nb)
    moe = i % 2 == 1
    j = i // 2
    x1, h2, comb, combt = _post_attention(
        yssd, proj, ym, yg, x, mod_i, mod_row, p["ssd_norm_g"][i], p["g_mix_post"][i], p["g_ffn_pre"][i],
        p["w_out"][i], p["w_router"][j] if moe else None, tm)
    if moe:
        sup = min(MOE_SUPER, b * l)
        f = _moe_ffn(h2.reshape(b * l, d), comb.reshape(b * l, LANES), combt,
                     p["moe_wg"][j], p["moe_wu"][j], p["moe_wd"][j], min(MOE_BLOCK, sup), sup)
        x2 = _gated_residual(f.reshape(b, l, d), x1, mod_i, mod_row, p["g_ffn_post"][i], tm)
    else:
        x2 = _dense_ffn(h2, x1, mod_i, mod_row, p["g_ffn_post"][i],
                        p["ffn_wg"][j], p["ffn_wu"][j], p["ffn_wd"][j], tm)
    return x2, (state,) + tuple(ops[6:])


def kernel(x_prompt, x_sample, state_ssd, cache_mla_ckv, cache_mla_krope, cache_gqa_k, cache_gqa_v, c, c_ctx, w_mod, b_mod, g_mix_pre, g_mix_post, g_ffn_pre, g_ffn_post, w_in, ssd_conv_w, ssd_conv_b, ssd_A_log, ssd_dt_bias, ssd_D, ssd_norm_g, mla_q_norm_g, mla_w_uq, mla_kv_norm_g, mla_w_ukv, gqa_q_norm_g, gqa_k_norm_g, w_out, ffn_w_gate, ffn_w_up, ffn_w_down, moe_w_router, moe_w_gate, moe_w_up, moe_w_down):
    depth = w_in.shape[0]
    d = x_prompt.shape[-1]
    n_dec = x_sample.shape[0]
    hpg = HEADS_PER_GROUP

    in_idx = _in_proj_columns()
    uq_idx, uk_idx, uv_idx = _mla_weight_columns()
    row2 = lambda a: a.reshape(depth, 1, a.shape[-1])
    def ssd_rows(fwd, bwd):
        cols = [fwd.reshape(depth, SSD_GROUPS, hpg), bwd.reshape(depth, SSD_GROUPS, hpg),
                jnp.zeros((depth, SSD_GROUPS, SSD_ROWS - 2 * hpg), F32)]
        return jnp.concatenate(cols, axis=-1)

    par = jnp.concatenate([ssd_rows(ssd_dt_bias[:, 0], ssd_dt_bias[:, 1]),
                           ssd_rows(ssd_A_log[:, 0], ssd_A_log[:, 1]),
                           ssd_rows(ssd_D, jnp.zeros_like(ssd_D))], axis=-1)
    par = jnp.broadcast_to(par[..., None], par.shape + (LANES,))
    zeros64 = jnp.zeros((depth, 1, GQA_HEAD_DIM), F32)
    attn_w = []
    for i in range(depth):
        attn_w.append((
            mla_q_norm_g[i][None, :], mla_kv_norm_g[i][None, :],
            _gather_cols(mla_w_uq[i], uq_idx).astype(BF16),
            _gather_cols(mla_w_ukv[i], uk_idx).astype(BF16),
            _gather_cols(mla_w_ukv[i], uv_idx).astype(BF16),
            jnp.concatenate([gqa_q_norm_g[i][None, :], zeros64[i]], axis=1),
            jnp.concatenate([gqa_k_norm_g[i][None, :]] * 2, axis=1),
        ))
    p = dict(
        g_mix_pre=row2(g_mix_pre), g_mix_post=row2(g_mix_post), g_ffn_pre=row2(g_ffn_pre),
        g_ffn_post=row2(g_ffn_post), ssd_norm_g=row2(ssd_norm_g),
        w_in=[_gather_cols(w_in[i].astype(BF16), in_idx) for i in range(depth)],
        conv_w=ssd_conv_w, conv_b=row2(ssd_conv_b), ssd_par=par, attn_w=attn_w,
        w_out=w_out.astype(BF16),
        ffn_wg=ffn_w_gate.astype(BF16), ffn_wu=ffn_w_up.astype(BF16), ffn_wd=ffn_w_down.astype(BF16),
        w_router=jnp.pad(moe_w_router, ((0, 0), (0, 0), (0, LANES - N_EXPERTS))),
        moe_wg=moe_w_gate.astype(BF16), moe_wu=moe_w_up.astype(BF16), moe_wd=moe_w_down.astype(BF16),
    )

    rows = -(-(1 + n_dec) // 8) * 8
    c_all = jnp.concatenate([c_ctx[None, :], c, jnp.zeros((rows - 1 - n_dec, d), F32)], axis=0)
    mod = _modulation(c_all, w_mod, b_mod).reshape(depth, rows, 1, 6 * d)

    y = x_prompt
    collected = [[], [], [], [], []]
    for i in range(depth):
        y, outs = _trunk_layer(i, y, mod[i], lambda bi: 0, p, None, None)
        for lst, t in zip(collected, outs):
            lst.append(t)
    y_prompt = y
    bsz, seq = x_prompt.shape[:2]
    new_state = jnp.stack(collected[0], axis=1)
    new_ckv = jnp.stack(collected[1], axis=1)
    new_krope = jnp.stack(collected[2], axis=1)
    new_k = jnp.stack(collected[3], axis=1).reshape(bsz, depth, seq, GQA_KV_HEADS, GQA_HEAD_DIM)
    new_v = jnp.stack(collected[4], axis=1).reshape(bsz, depth, seq, GQA_KV_HEADS, GQA_HEAD_DIM)

    past = cache_mla_ckv.shape[2]
    krope_blk = jnp.pad(cache_mla_krope, ((0, 0), (0, 0), (0, 0), (MLA_NOPE, LANES - MLA_NOPE - MLA_ROPE)))
    ctx = dict(
        state=state_ssd,
        kv=(cache_mla_ckv, krope_blk,
            cache_gqa_k.reshape(n_dec, depth, past, LANES), cache_gqa_v.reshape(n_dec, depth, past, LANES)),
    )
    tables = _rope_tables(x_sample.shape[1])
    y = x_sample
    for i in range(depth):
        y, _ = _trunk_layer(i, y, mod[i], lambda bi: bi + 1, p, tables, ctx)
    return (y_prompt, y, new_state, new_ckv, new_krope, new_k, new_v)
```

```python
import functools
import itertools
import math

import numpy as np
import jax
import jax.numpy as jnp
from jax import lax
from jax.experimental import pallas as pl
from jax.experimental.pallas import tpu as pltpu

F32 = jnp.float32
BF16 = jnp.bfloat16

EPS = 1e-6
ROPE_BASE = 10000.0
GRID_W = 64
SSD_HEAD_DIM = 64
SSD_HEADS = 8
SSD_GROUPS = 2
SSD_STATE = 128
SSD_CHUNK = 128
SSD_INNER = SSD_HEADS * SSD_HEAD_DIM
HEADS_PER_GROUP = SSD_HEADS // SSD_GROUPS
SSD_ROWS = 16
MLA_HEADS = 4
MLA_V = 64
MLA_NOPE = 64
MLA_ROPE = 32
MLA_Q_LORA = 256
MLA_KV_LORA = 128
GQA_HEADS = 4
GQA_KV_HEADS = 2
GQA_HEAD_DIM = 64
N_EXPERTS = 8

LANES = 128
BF16_SUBLANES = 16

COL_Z = 0
COL_X = 512
COL_B = 1024
COL_C = 1280
COL_MLA = 1536
COL_GQA = 2048
COL_DT = 3072
IN_COLS_PADDED = 3328

VMEM_LIMIT = 56 * 1024 * 1024

NT_DIMS = (((1,), (1,)), ((), ()))
TN_DIMS = (((0,), (0,)), ((), ()))


def _cparams(n_grid):
    return pltpu.CompilerParams(
        dimension_semantics=("arbitrary",) * n_grid, vmem_limit_bytes=VMEM_LIMIT)


def _const_spec(shape):
    nd = len(shape)
    return pl.BlockSpec(shape, lambda *_: (0,) * nd, pipeline_mode=pl.Buffered(1))


def _dot(a, b):
    return jnp.dot(a, b, preferred_element_type=F32)


def _rms(x, width=None):
    n = x.shape[-1] if width is None else width
    return x * lax.rsqrt(jnp.sum(x * x, axis=-1, keepdims=True) * (1.0 / n) + EPS)


def _silu(x):
    return x * jax.nn.sigmoid(x)


def _split3(v):
    hi = v.astype(BF16)
    r = v - hi.astype(F32)
    mid = r.astype(BF16)
    lo = (r - mid.astype(F32)).astype(BF16)
    return hi, mid, lo


def _mod_kernel(c_ref, w_ref, b_ref, o_ref):
    s = _silu(c_ref[...]).astype(BF16)
    o_ref[0] = _dot(s, w_ref[0].astype(BF16)) + b_ref[0]


def _modulation(c_all, w_mod, b_mod):
    depth, d, n = w_mod.shape
    tn = 1536
    rows = c_all.shape[0]
    return pl.pallas_call(
        _mod_kernel,
        grid=(depth, n // tn),
        in_specs=[
            pl.BlockSpec((rows, d), lambda i, j: (0, 0)),
            pl.BlockSpec((1, d, tn), lambda i, j: (i, 0, j)),
            pl.BlockSpec((1, 1, tn), lambda i, j: (i, 0, j)),
        ],
        out_specs=pl.BlockSpec((1, rows, tn), lambda i, j: (i, 0, j)),
        out_shape=jax.ShapeDtypeStruct((depth, rows, n), F32),
        compiler_params=_cparams(2),
        name="modulation",
    )(c_all, w_mod, b_mod.reshape(depth, 1, n))


def _inproj_kernel(x_ref, sh_ref, sc_ref, g_ref, w_ref, o_ref, dt_ref):
    h = _rms(x_ref[0]) * g_ref[...]
    h = h * (1.0 + sc_ref[0]) + sh_ref[0]
    proj = _dot(h.astype(BF16), w_ref[...])
    o_ref[0] = proj[:, 0:COL_DT].astype(BF16)
    dt_ref[0] = proj[:, COL_DT:]


def _in_projection(x, mod, mod_row, gain, w_in_p, tm):
    b, l, d = x.shape
    n = w_in_p.shape[1]
    n_dt = n - COL_DT
    return pl.pallas_call(
        _inproj_kernel,
        grid=(b, l // tm),
        in_specs=[
            pl.BlockSpec((1, tm, d), lambda i, t: (i, t, 0)),
            pl.BlockSpec((1, 1, d), lambda i, t: (mod_row(i), 0, 0)),
            pl.BlockSpec((1, 1, d), lambda i, t: (mod_row(i), 0, 1)),
            _const_spec((1, d)),
            _const_spec((d, n)),
        ],
        out_specs=[pl.BlockSpec((1, tm, COL_DT), lambda i, t: (i, t, 0)),
                   pl.BlockSpec((1, tm, n_dt), lambda i, t: (i, t, 0))],
        out_shape=[jax.ShapeDtypeStruct((b, l, COL_DT), BF16), jax.ShapeDtypeStruct((b, l, n_dt), F32)],
        compiler_params=_cparams(2),
        name="in_projection",
    )(x, mod, mod, gain, w_in_p)


def _ssd_kernel(*refs, nc, seq, has_h0, emit_state):
    it = iter(refs)
    x_ref, b_ref, c_ref, dt_ref = next(it), next(it), next(it), next(it)
    cwx_ref, cwb_ref, cwc_ref = next(it), next(it), next(it)
    cbx_ref, cbb_ref, cbc_ref = next(it), next(it), next(it)
    par_ref = next(it)
    h0_ref = next(it) if has_h0 else None
    y_ref = next(it)
    st_ref = next(it) if emit_state else None
    xt_s, yd_s, bc_s, dtv_s, arow_s, acol_s, h_s = (next(it) for _ in range(7))

    q = SSD_CHUNK
    hpg = HEADS_PER_GROUP
    hd = SSD_HEAD_DIM
    wx = hpg * hd
    r = SSD_ROWS
    pack = BF16_SUBLANES
    row = lax.broadcasted_iota(jnp.int32, (q, 1), 0)
    si = lax.broadcasted_iota(jnp.int32, (q, q), 0)
    ti = lax.broadcasted_iota(jnp.int32, (q, q), 1)
    le = si <= ti
    ge = si >= ti
    tri_le = jnp.where(le, 1.0, 0.0).astype(BF16)
    tri_ge = jnp.where(ge, 1.0, 0.0).astype(BF16)
    par = par_ref[0]
    bias_t, alog_t, d_t = par[0:r], par[r:2 * r], par[2 * r:3 * r]

    for d in range(2):
        if has_h0:
            h_s[d] = h0_ref[0, d].reshape(wx, SSD_STATE)
        else:
            h_s[d] = jnp.zeros((wx, SSD_STATE), F32)

    def conv_silu(ref, w_ref, bias_ref, c):
        off = pl.multiple_of(c * q, q)
        u = ref[0, pl.ds(off, q), :].astype(F32)
        before = ref[0, pl.ds(pl.multiple_of(jnp.maximum(off - pack, 0), pack), pack), :]
        after = ref[0, pl.ds(pl.multiple_of(jnp.minimum(off + q, seq - pack), pack), pack), :]
        prev = before[pack - 1:pack, :].astype(F32) * jnp.where(c > 0, 1.0, 0.0)
        nxt = after[0:1, :].astype(F32) * jnp.where(c < nc - 1, 1.0, 0.0)
        up = jnp.where(row == 0, prev, pltpu.roll(u, 1, 0))
        un = jnp.where(row == q - 1, nxt, pltpu.roll(u, q - 1, 0))
        w = w_ref[...]
        return _silu(up * w[0:1] + u * w[1:2] + un * w[2:3] + bias_ref[...])

    tri_rows = jnp.concatenate([tri_le, tri_ge], axis=1)
    tri_cols = jnp.concatenate([tri_ge, tri_le], axis=0)
    fwd_row = lax.broadcasted_iota(jnp.int32, (r, 1), 0) < hpg
    fwd_lane = lax.broadcasted_iota(jnp.int32, (1, r), 1) < hpg

    def prep_body(c, carry):
        off = pl.multiple_of(c * q, q)
        xt = conv_silu(x_ref, cwx_ref, cbx_ref, c).T
        xt_s[c] = xt
        bc_s[pl.ds(off, q), 0:SSD_STATE] = conv_silu(b_ref, cwb_ref, cbb_ref, c).astype(BF16)
        bc_s[pl.ds(off, q), SSD_STATE:2 * SSD_STATE] = conv_silu(c_ref, cwc_ref, cbc_ref, c).astype(BF16)
        dtr = dt_ref[0, pl.ds(off, q), :].T[0:r, :] + bias_t
        dtv = jnp.maximum(dtr, 0.0) + jnp.log1p(jnp.exp(-jnp.abs(dtr)))
        dtv_s[c] = dtv
        parts = _split3(dtv * (-jnp.exp(alog_t)))
        rows = sum(_dot(part, tri_rows) for part in parts)
        cols = sum(lax.dot_general(tri_cols, part, NT_DIMS, preferred_element_type=F32) for part in parts)
        arow_s[c] = jnp.where(fwd_row, rows[:, 0:q], rows[:, q:2 * q])
        acol_s[c] = jnp.where(fwd_lane, cols[0:q, :], cols[q:2 * q, :])
        return carry

    def chunk(c, d):
        off = pl.multiple_of(c * q, q)
        base, mask = (0, le) if d == 0 else (hpg, ge)
        xt = xt_s[c]
        b16 = bc_s[pl.ds(off, q), 0:SSD_STATE]
        c16 = bc_s[pl.ds(off, q), SSD_STATE:2 * SSD_STATE]
        dtv, a_rows, a_cols = dtv_s[c], arow_s[c], acol_s[c]
        tot = a_rows[:, q - 1:q] if d == 0 else a_rows[:, 0:1]
        st = lax.dot_general(b16, c16, NT_DIMS, preferred_element_type=F32)
        hs = h_s[d]
        yoff = lax.dot_general(hs.astype(BF16), c16, NT_DIMS, preferred_element_type=F32)
        yield
        outs, xdecs, hnew = [], [], []
        for hl in range(hpg):
            k = base + hl
            rs = slice(hl * hd, (hl + 1) * hd)
            a_t = a_rows[k:k + 1, :]
            tot_k = tot[k:k + 1, :]
            seg = a_t - a_cols[:, k:k + 1]
            w = (st * jnp.exp(jnp.where(mask, seg, -jnp.inf))).astype(BF16)
            xdt = xt[rs, :] * dtv[k:k + 1, :]
            outs.append(_dot(xdt.astype(BF16), w) + yoff[rs, :] * jnp.exp(a_t))
            xdecs.append((xdt * jnp.exp(tot_k - a_t)).astype(BF16))
            hnew.append(hs[rs, :] * jnp.exp(tot_k))
            yield
        h_s[d] = jnp.concatenate(hnew, axis=0) + _dot(jnp.concatenate(xdecs, axis=0), b16)
        yd_s[d, c] = jnp.concatenate(outs, axis=0)

    def scan_body(i, carry):
        for _ in itertools.zip_longest(chunk(i, 0), chunk(nc - 1 - i, 1)):
            pass
        return carry

    def out_body(c, carry):
        xt = xt_s[c]
        skip = [xt[hl * hd:(hl + 1) * hd, :] * d_t[hl:hl + 1, :] for hl in range(hpg)]
        yt = yd_s[0, c] + yd_s[1, c] + jnp.concatenate(skip, axis=0)
        y_ref[0, pl.ds(pl.multiple_of(c * q, q), q), :] = yt.T
        return carry

    lax.fori_loop(0, nc, prep_body, 0, unroll=4)
    lax.fori_loop(0, nc, scan_body, 0, unroll=2)
    lax.fori_loop(0, nc, out_body, 0, unroll=4)
    if emit_state:
        for d in range(2):
            st_ref[0, d] = h_s[d].reshape(hpg, SSD_HEAD_DIM, SSD_STATE)


def _ssd_mixer(proj, dt, conv_w, conv_b, par, h0, emit_state):
    b, l, _ = proj.shape
    g = SSD_GROUPS
    wx = HEADS_PER_GROUP * SSD_HEAD_DIM
    n = SSD_STATE
    nc = l // SSD_CHUNK
    has_h0 = h0 is not None

    def col(width, start):
        blk = start // width
        return pl.BlockSpec((1, l, width), lambda i, j: (i, 0, blk + j))

    def cw(width, start, rows):
        blk = start // width
        return pl.BlockSpec((rows, width), lambda i, j: (0, blk + j))

    in_specs = [
        col(wx, COL_X), col(n, COL_B), col(n, COL_C), col(LANES, 0),
        cw(wx, 0, 3), cw(n, SSD_INNER, 3), cw(n, SSD_INNER + g * n, 3),
        cw(wx, 0, 1), cw(n, SSD_INNER, 1), cw(n, SSD_INNER + g * n, 1),
        pl.BlockSpec((1, 3 * SSD_ROWS, LANES), lambda i, j: (j, 0, 0)),
    ]
    args = [proj, proj, proj, dt, conv_w, conv_w, conv_w, conv_b, conv_b, conv_b, par]
    state_spec = pl.BlockSpec((1, 2, HEADS_PER_GROUP, SSD_HEAD_DIM, n), lambda i, j: (i, 0, j, 0, 0))
    if has_h0:
        in_specs.append(state_spec)
        args.append(h0)
    out_specs = [pl.BlockSpec((1, l, wx), lambda i, j: (i, 0, j))]
    out_shape = [jax.ShapeDtypeStruct((b, l, SSD_INNER), F32)]
    if emit_state:
        out_specs.append(state_spec)
        out_shape.append(jax.ShapeDtypeStruct((b, 2, SSD_HEADS, SSD_HEAD_DIM, n), F32))
    outs = pl.pallas_call(
        functools.partial(_ssd_kernel, nc=nc, seq=l, has_h0=has_h0, emit_state=emit_state),
        grid=(b, g),
        in_specs=in_specs,
        out_specs=out_specs,
        out_shape=out_shape,
        scratch_shapes=[
            pltpu.VMEM((nc, wx, SSD_CHUNK), F32),
            pltpu.VMEM((2, nc, wx, SSD_CHUNK), F32),
            pltpu.VMEM((l, 2 * n), BF16),
            pltpu.VMEM((nc, SSD_ROWS, SSD_CHUNK), F32),
            pltpu.VMEM((nc, SSD_ROWS, SSD_CHUNK), F32),
            pltpu.VMEM((nc, SSD_CHUNK, SSD_ROWS), F32),
            pltpu.VMEM((2, wx, n), F32),
        ],
        compiler_params=_cparams(2),
        name="ssd_mixer",
    )(*args)
    return (outs[0], outs[1]) if emit_state else (outs[0], None)


def _rope(x, cos, sin_signed, pair):
    n = x.shape[1]
    lane = lax.broadcasted_iota(jnp.int32, (1, n), 1)
    first = (lane & (2 * pair - 1)) < pair
    partner = jnp.where(first, pltpu.roll(x, n - pair, 1), pltpu.roll(x, pair, 1))
    return x * cos + partner * sin_signed


def _tile_lanes(x, k):
    return jnp.concatenate([x] * k, axis=1)


def _qkv_kernel(*refs, nct, has_pos, emit_cache):
    it = iter(refs)
    mla_ref, gqa_ref = next(it), next(it)
    qg_ref, kvg_ref, wuq_ref, wk_ref, wv_ref, gq_ref, gk_ref = (next(it) for _ in range(7))
    if has_pos:
        mcos_ref, msin_ref, gcos_ref, gsin_ref = (next(it) for _ in range(4))
    if nct:
        cckv_ref, ckr_ref, cgk_ref, cgv_ref = (next(it) for _ in range(4))
    qm_ref, km_ref, vm_ref, qg_out, kg_ref, vg_ref = (next(it) for _ in range(6))
    if emit_cache:
        ckv_out, kpe_out, kn_out, v_out = (next(it) for _ in range(4))

    t = pl.program_id(1)
    lane = lax.broadcasted_iota(jnp.int32, (1, LANES), 1)
    first_half = lane < GQA_HEAD_DIM
    mla_scale = math.log2(math.e) / math.sqrt(MLA_NOPE + MLA_ROPE)
    gqa_scale = math.log2(math.e) / math.sqrt(GQA_HEAD_DIM)
    ones_lane = jnp.where(lane == GQA_HEAD_DIM, 1.0, 0.0)

    def write_mla_kv(ckv, kpe_r):
        c16 = ckv.astype(BF16)
        km_ref[0] = (_dot(c16, wk_ref[...]) + _tile_lanes(kpe_r, MLA_HEADS)).astype(BF16)
        vm_ref[0] = (_dot(c16, wv_ref[...]) + _tile_lanes(ones_lane, MLA_HEADS)).T.astype(BF16)

    def new_tile():
        m = mla_ref[0].astype(F32)
        q_lat = m[:, 0:MLA_Q_LORA]
        kv_lat = m[:, MLA_Q_LORA:MLA_Q_LORA + MLA_KV_LORA]
        kpe = m[:, MLA_Q_LORA + MLA_KV_LORA:]
        qn = _rms(q_lat) * qg_ref[...]
        qm = _dot(qn.astype(BF16), wuq_ref[...])
        ckv = _rms(kv_lat) * kvg_ref[...]
        kpe_r = kpe
        if has_pos:
            mcos, msin = mcos_ref[...], msin_ref[...]
            qm = _rope(qm, _tile_lanes(mcos, MLA_HEADS), _tile_lanes(msin, MLA_HEADS), MLA_ROPE // 4)
            kpe_r = _rope(kpe, mcos, msin, MLA_ROPE // 4)
        qm_ref[0] = (qm * mla_scale).astype(BF16)
        write_mla_kv(ckv, kpe_r)

        g = gqa_ref[0].astype(F32)
        if has_pos:
            gcos, gsin = gcos_ref[...], gsin_ref[...]
        for h in range(GQA_HEADS):
            xh = g[:, h * LANES:(h + 1) * LANES]
            qh = _rms(xh, GQA_HEAD_DIM) * gq_ref[...]
            if has_pos:
                qh = _rope(qh, gcos, gsin, GQA_HEAD_DIM // 4)
            qg_out[0, :, h * LANES:(h + 1) * LANES] = (qh * gqa_scale).astype(BF16)
        kns = []
        for j in range(GQA_KV_HEADS):
            c0 = GQA_HEADS * LANES + j * LANES
            kn = _rms(g[:, c0:c0 + LANES]) * gk_ref[...]
            kns.append(kn)
            kr = _rope(kn, gcos, gsin, GQA_HEAD_DIM // 4) if has_pos else kn
            kg_ref[0, :, j * LANES:(j + 1) * LANES] = kr.astype(BF16)
        v0 = (GQA_HEADS + GQA_KV_HEADS) * LANES
        v = g[:, v0:v0 + GQA_KV_HEADS * LANES]
        for j in range(GQA_KV_HEADS):
            vj = jnp.where(first_half, v[:, j * LANES:(j + 1) * LANES], ones_lane)
            vg_ref[0, j * LANES:(j + 1) * LANES, :] = vj.T.astype(BF16)
        if emit_cache:
            ckv_out[0] = ckv
            kpe_out[0] = kpe[:, MLA_NOPE:MLA_NOPE + MLA_ROPE]
            kn_out[0] = jnp.where(first_half, kns[0], kns[1])
            v_out[0] = jnp.where(first_half, v[:, 0:LANES], v[:, LANES:2 * LANES])

    def ctx_tile():
        write_mla_kv(cckv_ref[0], ckr_ref[0])
        k = cgk_ref[0]
        k_sw = pltpu.roll(k, GQA_HEAD_DIM, 1)
        kg_ref[0, :, 0:LANES] = jnp.where(first_half, k, k_sw).astype(BF16)
        kg_ref[0, :, LANES:2 * LANES] = jnp.where(first_half, k_sw, k).astype(BF16)
        v = cgv_ref[0]
        vg_ref[0, 0:LANES, :] = jnp.where(first_half, v, ones_lane).T.astype(BF16)
        vg_ref[0, LANES:2 * LANES, :] = jnp.where(first_half, pltpu.roll(v, GQA_HEAD_DIM, 1), ones_lane).T.astype(BF16)

    if nct:
        pl.when(t < nct)(ctx_tile)
        pl.when(t >= nct)(new_tile)
    else:
        new_tile()


def _attention_operands(proj, wts, tables, ctx, layer, emit_cache, tm):
    b, l, _ = proj.shape
    has_pos = tables is not None
    nct = 0 if ctx is None else ctx[0].shape[2] // tm
    lk = l + nct * tm
    nt = lk // tm

    def new_t(t):
        return jnp.maximum(t - nct, 0) if nct else t

    in_specs = [
        pl.BlockSpec((1, tm, 512), lambda i, t: (i, new_t(t), COL_MLA // 512)),
        pl.BlockSpec((1, tm, 1024), lambda i, t: (i, new_t(t), COL_GQA // 1024)),
    ] + [_const_spec(w.shape) for w in wts]
    args = [proj, proj] + list(wts)
    if has_pos:
        in_specs += [pl.BlockSpec((tm, LANES), lambda i, t: (new_t(t), 0))] * 4
        args += list(tables)
    if nct:
        in_specs += [pl.BlockSpec((1, None, tm, LANES), lambda i, t: (i, layer, jnp.minimum(t, nct - 1), 0))] * 4
        args += list(ctx)
    q_spec = lambda w: pl.BlockSpec((1, tm, w), lambda i, t: (i, new_t(t), 0))
    k_spec = lambda w: pl.BlockSpec((1, tm, w), lambda i, t: (i, t, 0))
    vt_spec = lambda w: pl.BlockSpec((1, w, tm), lambda i, t: (i, 0, t))
    out_specs = [q_spec(512), k_spec(512), vt_spec(512), q_spec(512), k_spec(256), vt_spec(256)]
    out_shape = [
        jax.ShapeDtypeStruct((b, l, 512), BF16), jax.ShapeDtypeStruct((b, lk, 512), BF16),
        jax.ShapeDtypeStruct((b, 512, lk), BF16), jax.ShapeDtypeStruct((b, l, 512), BF16),
        jax.ShapeDtypeStruct((b, lk, 256), BF16), jax.ShapeDtypeStruct((b, 256, lk), BF16),
    ]
    if emit_cache:
        out_specs += [q_spec(MLA_KV_LORA), q_spec(MLA_ROPE), q_spec(LANES), q_spec(LANES)]
        out_shape += [
            jax.ShapeDtypeStruct((b, l, MLA_KV_LORA), F32), jax.ShapeDtypeStruct((b, l, MLA_ROPE), F32),
            jax.ShapeDtypeStruct((b, l, LANES), F32), jax.ShapeDtypeStruct((b, l, LANES), F32),
        ]
    return pl.pallas_call(
        functools.partial(_qkv_kernel, nct=nct, has_pos=has_pos, emit_cache=emit_cache),
        grid=(b, nt),
        in_specs=in_specs,
        out_specs=out_specs,
        out_shape=out_shape,
        compiler_params=_cparams(2),
        name="attention_operands",
    )(*args)


def _attn_kernel(q_ref, k_ref, vt_ref, o_ref, *, shared_kv):
    lane = lax.broadcasted_iota(jnp.int32, (1, LANES), 1)
    heads = (0, 1)
    cols = [slice(0, LANES) if shared_kv else slice(a * LANES, (a + 1) * LANES) for a in heads]
    tsub = min(ATTN_SUBTILE, q_ref.shape[1])
    units = [(i, r) for i in range(q_ref.shape[0]) for r in range(0, q_ref.shape[1], tsub)]

    def scores(u):
        i, r = u
        return [lax.dot_general(k_ref[i, :, cols[a]], q_ref[i, r:r + tsub, a * LANES:(a + 1) * LANES], NT_DIMS,
                                preferred_element_type=F32) for a in heads]

    def finish(u, sts):
        i, r = u
        pts = []
        for st in sts:
            part = jnp.max(st.reshape(-1, 2 * LANES, st.shape[1]), axis=0)
            pts.append(jnp.exp2(st - jnp.max(part, axis=0, keepdims=True)).astype(BF16))
        accs = [_dot(vt_ref[i, cols[a], :], pts[a]) for a in heads]
        outs = [(acc / acc[GQA_HEAD_DIM:GQA_HEAD_DIM + 1, :]).T for acc in accs]
        o_ref[i, r:r + tsub, :] = jnp.where(
            lane < GQA_HEAD_DIM, outs[0], pltpu.roll(outs[1], GQA_HEAD_DIM, 1)).astype(BF16)

    sts = scores(units[0])
    for n, u in enumerate(units):
        nxt = scores(units[n + 1]) if n + 1 < len(units) else None
        finish(u, sts)
        sts = nxt


def _attention(q, k, v, shared_kv, tq, nb):
    b, l, _ = q.shape
    lk = k.shape[1]
    kw = LANES if shared_kv else 2 * LANES
    return pl.pallas_call(
        functools.partial(_attn_kernel, shared_kv=shared_kv),
        grid=(b // nb, 2, l // tq),
        in_specs=[
            pl.BlockSpec((nb, tq, 2 * LANES), lambda i, j, t: (i, t, j)),
            pl.BlockSpec((nb, lk, kw), lambda i, j, t: (i, 0, j)),
            pl.BlockSpec((nb, kw, lk), lambda i, j, t: (i, j, 0)),
        ],
        out_specs=pl.BlockSpec((nb, tq, LANES), lambda i, j, t: (i, t, j)),
        out_shape=jax.ShapeDtypeStruct((b, l, 2 * LANES), BF16),
        compiler_params=_cparams(3),
        name="attention_shared_kv" if shared_kv else "attention",
    )(q, k, v)


def _post_kernel(*refs, moe):
    it = iter(refs)
    ys_ref, z_ref, ym_ref, yg_ref, x_ref, g1_ref, sh2_ref, sc2_ref = (next(it) for _ in range(8))
    ng_ref, gpost_ref, gpre_ref, wout_ref = (next(it) for _ in range(4))
    wr_ref = next(it) if moe else None
    x1_ref, h2_ref = next(it), next(it)
    comb_ref, combt_ref = (next(it), next(it)) if moe else (None, None)

    tm = x_ref.shape[1]
    if moe:
        w_hi = wr_ref[...].astype(BF16)
        w_lo = (wr_ref[...] - w_hi.astype(F32)).astype(BF16)
        lane = lax.broadcasted_iota(jnp.int32, (1, LANES), 1).astype(F32)

    def rows(r0, r1):
        y = _rms(ys_ref[0, r0:r1, :] * _silu(z_ref[0, r0:r1, :].astype(F32))) * ng_ref[...]
        cat = jnp.concatenate([y.astype(BF16), ym_ref[0, r0:r1, :], yg_ref[0, r0:r1, :]], axis=1)
        yield
        mix = _dot(cat, wout_ref[...])
        yield
        x1 = x_ref[0, r0:r1, :] + g1_ref[0] * (_rms(mix) * gpost_ref[...])
        x1_ref[0, r0:r1, :] = x1
        h2 = _rms(x1) * gpre_ref[...] * (1.0 + sc2_ref[0]) + sh2_ref[0]
        h16 = h2.astype(BF16)
        h2_ref[0, r0:r1, :] = h16
        if moe:
            yield
            logits = _dot(h16, w_hi) + _dot(h16, w_lo)
            yield
            lg = jnp.where(lane < N_EXPERTS, logits, -jnp.inf)
            m1 = jnp.max(lg, axis=-1, keepdims=True)
            i1 = jnp.min(jnp.where(lg == m1, lane, float(LANES)), axis=-1, keepdims=True)
            lg2 = jnp.where(lane == i1, -jnp.inf, lg)
            m2 = jnp.max(lg2, axis=-1, keepdims=True)
            i2 = jnp.min(jnp.where(lg2 == m2, lane, float(LANES)), axis=-1, keepdims=True)
            e = jnp.exp(m2 - m1)
            comb = jnp.where(lane == i1, 1.0 / (1.0 + e), jnp.where(lane == i2, e / (1.0 + e), 0.0))
            comb_ref[0, r0:r1, :] = comb
            combt_ref[:, r0:r1] = comb.T[0:N_EXPERTS, :]

    half = tm // 2 if tm % 256 == 0 else tm
    pending = [rows(r, r + half) for r in range(0, tm, half)]
    live = []
    while pending or live:
        if pending:
            live.append(pending.pop(0))
        for g in list(live):
            if next(g, StopIteration) is StopIteration:
                live.remove(g)


def _post_attention(yssd, proj, ym, yg, x, mod, mod_row, ng, gpost, gpre, w_out, w_router, tm):
    b, l, d = x.shape
    moe = w_router is not None
    tok = lambda w: pl.BlockSpec((1, tm, w), lambda i, t: (i, t, 0))
    modk = lambda k: pl.BlockSpec((1, 1, d), lambda i, t: (mod_row(i), 0, k))
    in_specs = [tok(SSD_INNER), tok(SSD_INNER), tok(256), tok(256), tok(d), modk(2), modk(3), modk(4),
                _const_spec(ng.shape), _const_spec(gpost.shape), _const_spec(gpre.shape),
                _const_spec(w_out.shape)]
    args = [yssd, proj, ym, yg, x, mod, mod, mod, ng, gpost, gpre, w_out]
    out_specs = [tok(d), tok(d)]
    out_shape = [jax.ShapeDtypeStruct((b, l, d), F32), jax.ShapeDtypeStruct((b, l, d), BF16)]
    if moe:
        in_specs.append(_const_spec(w_router.shape))
        args.append(w_router)
        out_specs += [tok(LANES), pl.BlockSpec((N_EXPERTS, tm), lambda i, t: (0, i * (l // tm) + t))]
        out_shape += [jax.ShapeDtypeStruct((b, l, LANES), F32), jax.ShapeDtypeStruct((N_EXPERTS, b * l), F32)]
    outs = pl.pallas_call(
        functools.partial(_post_kernel, moe=moe),
        grid=(b, l // tm),
        in_specs=in_specs,
        out_specs=out_specs,
        out_shape=out_shape,
        compiler_params=_cparams(2),
        name="post_attention",
    )(*args)
    return outs if moe else (outs[0], outs[1], None, None)


def _ffn_kernel(h_ref, x1_ref, g2_ref, gp_ref, wg_ref, wu_ref, wd_ref, o_ref):
    h = h_ref[0]
    act = (_silu(_dot(h, wg_ref[...])) * _dot(h, wu_ref[...])).astype(BF16)
    f = _dot(act, wd_ref[...])
    o_ref[0] = x1_ref[0] + g2_ref[0] * (_rms(f) * gp_ref[...])


def _dense_ffn(h2, x1, mod, mod_row, gp, wg, wu, wd, tm):
    b, l, d = x1.shape
    tok = pl.BlockSpec((1, tm, d), lambda i, t: (i, t, 0))
    return pl.pallas_call(
        _ffn_kernel,
        grid=(b, l // tm),
        in_specs=[tok, tok, pl.BlockSpec((1, 1, d), lambda i, t: (mod_row(i), 0, 5)),
                  _const_spec(gp.shape), _const_spec(wg.shape), _const_spec(wu.shape), _const_spec(wd.shape)],
        out_specs=tok,
        out_shape=jax.ShapeDtypeStruct((b, l, d), F32),
        compiler_params=_cparams(2),
        name="dense_ffn",
    )(h2, x1, mod, gp, wg, wu, wd)


ATTN_TILE = 2048
ATTN_SUBTILE = 256

MOE_BLOCK = 1024
MOE_CAP = 256
MOE_SUPER = 2048


def _moe_kernel(h_ref, comb_ref, combt_ref, wg_ref, wu_ref, wd_ref, o_ref, pos_s, *, tb, cap):
    e = pl.program_id(1)
    nsub = o_ref.shape[0] // tb

    @pl.when(e == 0)
    def _():
        o_ref[...] = jnp.zeros(o_ref.shape, F32)
        si = lax.broadcasted_iota(jnp.int32, (tb, tb), 0)
        ti = lax.broadcasted_iota(jnp.int32, (tb, tb), 1)
        before = jnp.where(si < ti, 1.0, 0.0).astype(BF16)
        for u in range(nsub):
            sel = jnp.where(combt_ref[:, u * tb:(u + 1) * tb] > 0.0, 1.0, 0.0).astype(BF16)
            pos_s[:, u * tb:(u + 1) * tb] = _dot(sel, before)

    lane = lax.broadcasted_iota(jnp.int32, (1, LANES), 1)
    half = cap // 2

    def block(u, carry):
        off = pl.multiple_of(u * tb, tb)
        sel_e = combt_ref[pl.ds(e, 1), pl.ds(off, tb)] > 0.0
        pos_e = pos_s[pl.ds(e, 1), pl.ds(off, tb)]
        n_e = jnp.sum(jnp.where(sel_e, 1.0, 0.0)).astype(jnp.int32)
        gate = jnp.sum(jnp.where(lane == e, comb_ref[pl.ds(off, tb), :], 0.0), axis=-1, keepdims=True)

        def tile(first, rows):
            slot = lax.broadcasted_iota(jnp.int32, (rows, 1), 0).astype(F32)
            pick = jnp.where(sel_e & (pos_e - first == slot), 1.0, 0.0).astype(BF16)
            xg = _dot(pick, h_ref[pl.ds(off, tb), :]).astype(BF16)
            act = (_silu(_dot(xg, wg_ref[0])) * _dot(xg, wu_ref[0])).astype(BF16)
            y = _dot(act, wd_ref[0]).astype(BF16)
            o_ref[pl.ds(off, tb), :] += gate * lax.dot_general(pick, y, TN_DIMS, preferred_element_type=F32)

        def full_tile(j, carry):
            tile(j * float(cap), cap)
            return carry

        n_full = (n_e + half - 1) // cap
        lax.fori_loop(0, n_full, full_tile, 0)
        pl.when(n_e > n_full * cap)(lambda: tile(n_full * float(cap), half))
        return carry

    lax.fori_loop(0, nsub, block, 0)


def _moe_ffn(h2, comb, combt, wg, wu, wd, tb, sup):
    t, d = h2.shape
    n_e, _, ff = wg.shape
    cap = -(-(MOE_CAP * tb // MOE_BLOCK) // 16) * 16
    once = pl.Buffered(1)
    return pl.pallas_call(
        functools.partial(_moe_kernel, tb=tb, cap=cap),
        grid=(t // sup, n_e),
        in_specs=[pl.BlockSpec((sup, d), lambda s, e: (s, 0), pipeline_mode=once),
                  pl.BlockSpec((sup, LANES), lambda s, e: (s, 0), pipeline_mode=once),
                  pl.BlockSpec((N_EXPERTS, sup), lambda s, e: (0, s), pipeline_mode=once),
                  pl.BlockSpec((1, d, ff), lambda s, e: (e, 0, 0)),
                  pl.BlockSpec((1, d, ff), lambda s, e: (e, 0, 0)),
                  pl.BlockSpec((1, ff, d), lambda s, e: (e, 0, 0))],
        out_specs=pl.BlockSpec((sup, d), lambda s, e: (s, 0), pipeline_mode=once),
        out_shape=jax.ShapeDtypeStruct((t, d), F32),
        scratch_shapes=[pltpu.VMEM((N_EXPERTS, sup), F32)],
        compiler_params=_cparams(2),
        name="moe_ffn",
    )(h2, comb, combt, wg, wu, wd)


def _residual_kernel(f_ref, x1_ref, g2_ref, gp_ref, o_ref):
    o_ref[0] = x1_ref[0] + g2_ref[0] * (_rms(f_ref[0]) * gp_ref[...])


def _gated_residual(f, x1, mod, mod_row, gp, tm):
    b, l, d = x1.shape
    tok = pl.BlockSpec((1, tm, d), lambda i, t: (i, t, 0))
    return pl.pallas_call(
        _residual_kernel,
        grid=(b, l // tm),
        in_specs=[tok, tok, pl.BlockSpec((1, 1, d), lambda i, t: (mod_row(i), 0, 5)), _const_spec(gp.shape)],
        out_specs=tok,
        out_shape=jax.ShapeDtypeStruct((b, l, d), F32),
        compiler_params=_cparams(2),
        name="gated_residual",
    )(f, x1, mod, gp)


def _in_proj_columns():
    o_dt = 1536
    o_ql, o_kv, o_kpe = 1552, 1808, 1936
    o_gq, o_gk, o_gv = 1968, 2224, 2352
    idx = np.full((IN_COLS_PADDED,), -1, np.int64)
    idx[0:1536] = np.arange(1536)
    idx[COL_MLA:COL_MLA + 256] = o_ql + np.arange(256)
    idx[COL_MLA + 256:COL_MLA + 384] = o_kv + np.arange(128)
    idx[COL_MLA + 384 + MLA_NOPE:COL_MLA + 384 + MLA_NOPE + MLA_ROPE] = o_kpe + np.arange(MLA_ROPE)
    for h in range(GQA_HEADS):
        idx[COL_GQA + h * LANES:COL_GQA + h * LANES + 64] = o_gq + h * 64 + np.arange(64)
    for j in range(GQA_KV_HEADS):
        for r in range(2):
            c0 = COL_GQA + 512 + j * LANES + r * 64
            idx[c0:c0 + 64] = o_gk + j * 64 + np.arange(64)
            idx[c0 + 256:c0 + 256 + 64] = o_gv + j * 64 + np.arange(64)
    for g in range(SSD_GROUPS):
        for d in range(2):
            c0 = COL_DT + g * LANES + d * HEADS_PER_GROUP
            idx[c0:c0 + HEADS_PER_GROUP] = o_dt + d * SSD_HEADS + g * HEADS_PER_GROUP + np.arange(HEADS_PER_GROUP)
    return idx


def _gather_cols(w, idx):
    cols = jnp.take(w, jnp.asarray(np.maximum(idx, 0)), axis=1)
    return jnp.where(jnp.asarray(idx >= 0)[None, :], cols, 0.0)


def _mla_weight_columns():
    per_q = MLA_NOPE + MLA_ROPE
    uq = np.full((MLA_HEADS * LANES,), -1, np.int64)
    uk = np.full((MLA_HEADS * LANES,), -1, np.int64)
    uv = np.full((MLA_HEADS * LANES,), -1, np.int64)
    for h in range(MLA_HEADS):
        uq[h * LANES:h * LANES + per_q] = h * per_q + np.arange(per_q)
        uk[h * LANES:h * LANES + MLA_NOPE] = h * (MLA_NOPE + MLA_V) + np.arange(MLA_NOPE)
        uv[h * LANES:h * LANES + MLA_V] = h * (MLA_NOPE + MLA_V) + MLA_NOPE + np.arange(MLA_V)
    return uq, uk, uv


def _rope_tables(n_tokens):
    t = np.arange(n_tokens)
    row, colp = (t // GRID_W).astype(np.float64), (t % GRID_W).astype(np.float64)

    def table(width, lane0, head_dim, reps):
        cos = np.ones((n_tokens, width))
        sin = np.zeros((n_tokens, width))
        half = head_dim // 2
        quarter = half // 2
        for dim in range(head_dim):
            pos = row if dim < half else colp
            inv = ROPE_BASE ** (-(2.0 * (dim % quarter)) / half)
            ang = pos * inv
            sign = -1.0 if (dim % half) < quarter else 1.0
            for r in range(reps):
                cos[:, lane0 + r * head_dim + dim] = np.cos(ang)
                sin[:, lane0 + r * head_dim + dim] = sign * np.sin(ang)
        return jnp.asarray(cos, F32), jnp.asarray(sin, F32)

    mcos, msin = table(LANES, MLA_NOPE, MLA_ROPE, 1)
    gcos, gsin = table(LANES, 0, GQA_HEAD_DIM, 2)
    return mcos, msin, gcos, gsin


def _trunk_layer(i, x, mod_i, mod_row, p, tables, ctx):
    b, l, d = x.shape
    tm = min(512, l)
    emit = ctx is None
    proj, dt = _in_projection(x, mod_i, mod_row, p["g_mix_pre"][i], p["w_in"][i], tm)
    h0 = None if ctx is None else ctx["state"][:, i]
    yssd, state = _ssd_mixer(proj, dt, p["conv_w"][i], p["conv_b"][i], p["ssd_par"][i], h0, emit)
    ctx_kv = None if ctx is None else ctx["kv"]
    ops = _attention_operands(proj, p["attn_w"][i], tables, ctx_kv, i, emit, tm)
    qm, km, vm, qg, kg, vg = ops[:6]
    tq = min(ATTN_TILE, l)
    nb = math.gcd(b, max(1, 1024 // km.shape[1]))
    ym = _attention(qm, km, vm, False, tq, nb)
    yg = _attention(qg, kg, vg, True, tq, nb)
    moe = i % 2 == 1
    j = i // 2
    x1, h2, comb, combt = _post_attention(
        yssd, proj, ym, yg, x, mod_i, mod_row, p["ssd_norm_g"][i], p["g_mix_post"][i], p["g_ffn_pre"][i],
        p["w_out"][i], p["w_router"][j] if moe else None, tm)
    if moe:
        sup = min(MOE_SUPER, b * l)
        f = _moe_ffn(h2.reshape(b * l, d), comb.reshape(b * l, LANES), combt,
                     p["moe_wg"][j], p["moe_wu"][j], p["moe_wd"][j], min(MOE_BLOCK, sup), sup)
        x2 = _gated_residual(f.reshape(b, l, d), x1, mod_i, mod_row, p["g_ffn_post"][i], tm)
    else:
        x2 = _dense_ffn(h2, x1, mod_i, mod_row, p["g_ffn_post"][i],
                        p["ffn_wg"][j], p["ffn_wu"][j], p["ffn_wd"][j], tm)
    return x2, (state,) + tuple(ops[6:])


def kernel(x_prompt, x_sample, state_ssd, cache_mla_ckv, cache_mla_krope, cache_gqa_k, cache_gqa_v, c, c_ctx, w_mod, b_mod, g_mix_pre, g_mix_post, g_ffn_pre, g_ffn_post, w_in, ssd_conv_w, ssd_conv_b, ssd_A_log, ssd_dt_bias, ssd_D, ssd_norm_g, mla_q_norm_g, mla_w_uq, mla_kv_norm_g, mla_w_ukv, gqa_q_norm_g, gqa_k_norm_g, w_out, ffn_w_gate, ffn_w_up, ffn_w_down, moe_w_router, moe_w_gate, moe_w_up, moe_w_down):
    depth = w_in.shape[0]
    d = x_prompt.shape[-1]
    n_dec = x_sample.shape[0]
    hpg = HEADS_PER_GROUP

    in_idx = _in_proj_columns()
    uq_idx, uk_idx, uv_idx = _mla_weight_columns()
    row2 = lambda a: a.reshape(depth, 1, a.shape[-1])
    def ssd_rows(fwd, bwd):
        cols = [fwd.reshape(depth, SSD_GROUPS, hpg), bwd.reshape(depth, SSD_GROUPS, hpg),
                jnp.zeros((depth, SSD_GROUPS, SSD_ROWS - 2 * hpg), F32)]
        return jnp.concatenate(cols, axis=-1)

    par = jnp.concatenate([ssd_rows(ssd_dt_bias[:, 0], ssd_dt_bias[:, 1]),
                           ssd_rows(ssd_A_log[:, 0], ssd_A_log[:, 1]),
                           ssd_rows(ssd_D, jnp.zeros_like(ssd_D))], axis=-1)
    par = jnp.broadcast_to(par[..., None], par.shape + (LANES,))
    zeros64 = jnp.zeros((depth, 1, GQA_HEAD_DIM), F32)
    attn_w = []
    for i in range(depth):
        attn_w.append((
            mla_q_norm_g[i][None, :], mla_kv_norm_g[i][None, :],
            _gather_cols(mla_w_uq[i], uq_idx).astype(BF16),
            _gather_cols(mla_w_ukv[i], uk_idx).astype(BF16),
            _gather_cols(mla_w_ukv[i], uv_idx).astype(BF16),
            jnp.concatenate([gqa_q_norm_g[i][None, :], zeros64[i]], axis=1),
            jnp.concatenate([gqa_k_norm_g[i][None, :]] * 2, axis=1),
        ))
    p = dict(
        g_mix_pre=row2(g_mix_pre), g_mix_post=row2(g_mix_post), g_ffn_pre=row2(g_ffn_pre),
        g_ffn_post=row2(g_ffn_post), ssd_norm_g=row2(ssd_norm_g),
        w_in=[_gather_cols(w_in[i].astype(BF16), in_idx) for i in range(depth)],
        conv_w=ssd_conv_w, conv_b=row2(ssd_conv_b), ssd_par=par, attn_w=attn_w,
        w_out=w_out.astype(BF16),
        ffn_wg=ffn_w_gate.astype(BF16), ffn_wu=ffn_w_up.astype(BF16), ffn_wd=ffn_w_down.astype(BF16),
        w_router=jnp.pad(moe_w_router, ((0, 0), (0, 0), (0, LANES - N_EXPERTS))),
        moe_wg=moe_w_gate.astype(BF16), moe_wu=moe_w_up.astype(BF16), moe_wd=moe_w_down.astype(BF16),
    )

    rows = -(-(1 + n_dec) // 8) * 8
    c_all = jnp.concatenate([c_ctx[None, :], c, jnp.zeros((rows - 1 - n_dec, d), F32)], axis=0)
    mod = _modulation(c_all, w_mod, b_mod).reshape(depth, rows, 1, 6 * d)

    y = x_prompt
    collected = [[], [], [], [], []]
    for i in range(depth):
        y, outs = _trunk_layer(i, y, mod[i], lambda bi: 0, p, None, None)
        for lst, t in zip(collected, outs):
            lst.append(t)
    y_prompt = y
    bsz, seq = x_prompt.shape[:2]
    new_state = jnp.stack(collected[0], axis=1)
    new_ckv = jnp.stack(collected[1], axis=1)
    new_krope = jnp.stack(collected[2], axis=1)
    new_k = jnp.stack(collected[3], axis=1).reshape(bsz, depth, seq, GQA_KV_HEADS, GQA_HEAD_DIM)
    new_v = jnp.stack(collected[4], axis=1).reshape(bsz, depth, seq, GQA_KV_HEADS, GQA_HEAD_DIM)

    past = cache_mla_ckv.shape[2]
    krope_blk = jnp.pad(cache_mla_krope, ((0, 0), (0, 0), (0, 0), (MLA_NOPE, LANES - MLA_NOPE - MLA_ROPE)))
    ctx = dict(
        state=state_ssd,
        kv=(cache_mla_ckv, krope_blk,
            cache_gqa_k.reshape(n_dec, depth, past, LANES), cache_gqa_v.reshape(n_dec, depth, past, LANES)),
    )
    tables = _rope_tables(x_sample.shape[1])
    y = x_sample
    for i in range(depth):
        y, _ = _trunk_layer(i, y, mod[i], lambda bi: bi + 1, p, tables, ctx)
    return (y_prompt, y, new_state, new_ckv, new_krope, new_k, new_v)
```

```python
import functools
import itertools
import math

import numpy as np
import jax
import jax.numpy as jnp
from jax import lax
from jax.experimental import pallas as pl
from jax.experimental.pallas import tpu as pltpu

F32 = jnp.float32
BF16 = jnp.bfloat16

EPS = 1e-6
ROPE_BASE = 10000.0
GRID_W = 64
SSD_HEAD_DIM = 64
SSD_HEADS = 8
SSD_GROUPS = 2
SSD_STATE = 128
SSD_CHUNK = 128
SSD_INNER = SSD_HEADS * SSD_HEAD_DIM
HEADS_PER_GROUP = SSD_HEADS // SSD_GROUPS
SSD_ROWS = 16
MLA_HEADS = 4
MLA_V = 64
MLA_NOPE = 64
MLA_ROPE = 32
MLA_Q_LORA = 256
MLA_KV_LORA = 128
GQA_HEADS = 4
GQA_KV_HEADS = 2
GQA_HEAD_DIM = 64
N_EXPERTS = 8

LANES = 128
BF16_SUBLANES = 16

COL_Z = 0
COL_X = 512
COL_B = 1024
COL_C = 1280
COL_MLA = 1536
COL_GQA = 2048
COL_DT = 3072
IN_COLS_PADDED = 3328

VMEM_LIMIT = 56 * 1024 * 1024

NT_DIMS = (((1,), (1,)), ((), ()))
TN_DIMS = (((0,), (0,)), ((), ()))


def _cparams(n_grid):
    return pltpu.CompilerParams(
        dimension_semantics=("arbitrary",) * n_grid, vmem_limit_bytes=VMEM_LIMIT)


def _const_spec(shape):
    nd = len(shape)
    return pl.BlockSpec(shape, lambda *_: (0,) * nd, pipeline_mode=pl.Buffered(1))


def _dot(a, b):
    return jnp.dot(a, b, preferred_element_type=F32)


def _rms(x, width=None):
    n = x.shape[-1] if width is None else width
    return x * lax.rsqrt(jnp.sum(x * x, axis=-1, keepdims=True) * (1.0 / n) + EPS)


def _silu(x):
    return x * jax.nn.sigmoid(x)


def _split3(v):
    hi = v.astype(BF16)
    r = v - hi.astype(F32)
    mid = r.astype(BF16)
    lo = (r - mid.astype(F32)).astype(BF16)
    return hi, mid, lo


def _mod_kernel(c_ref, w_ref, b_ref, o_ref):
    s = _silu(c_ref[...]).astype(BF16)
    o_ref[0] = _dot(s, w_ref[0].astype(BF16)) + b_ref[0]


def _modulation(c_all, w_mod, b_mod):
    depth, d, n = w_mod.shape
    tn = 1536
    rows = c_all.shape[0]
    return pl.pallas_call(
        _mod_kernel,
        grid=(depth, n // tn),
        in_specs=[
            pl.BlockSpec((rows, d), lambda i, j: (0, 0)),
            pl.BlockSpec((1, d, tn), lambda i, j: (i, 0, j)),
            pl.BlockSpec((1, 1, tn), lambda i, j: (i, 0, j)),
        ],
        out_specs=pl.BlockSpec((1, rows, tn), lambda i, j: (i, 0, j)),
        out_shape=jax.ShapeDtypeStruct((depth, rows, n), F32),
        compiler_params=_cparams(2),
        name="modulation",
    )(c_all, w_mod, b_mod.reshape(depth, 1, n))


def _inproj_kernel(x_ref, sh_ref, sc_ref, g_ref, w_ref, o_ref, dt_ref):
    h = _rms(x_ref[0]) * g_ref[...]
    h = h * (1.0 + sc_ref[0]) + sh_ref[0]
    proj = _dot(h.astype(BF16), w_ref[...])
    o_ref[0] = proj[:, 0:COL_DT].astype(BF16)
    dt_ref[0] = proj[:, COL_DT:]


def _in_projection(x, mod, mod_row, gain, w_in_p, tm):
    b, l, d = x.shape
    n = w_in_p.shape[1]
    n_dt = n - COL_DT
    return pl.pallas_call(
        _inproj_kernel,
        grid=(b, l // tm),
        in_specs=[
            pl.BlockSpec((1, tm, d), lambda i, t: (i, t, 0)),
            pl.BlockSpec((1, 1, d), lambda i, t: (mod_row(i), 0, 0)),
            pl.BlockSpec((1, 1, d), lambda i, t: (mod_row(i), 0, 1)),
            _const_spec((1, d)),
            _const_spec((d, n)),
        ],
        out_specs=[pl.BlockSpec((1, tm, COL_DT), lambda i, t: (i, t, 0)),
                   pl.BlockSpec((1, tm, n_dt), lambda i, t: (i, t, 0))],
        out_shape=[jax.ShapeDtypeStruct((b, l, COL_DT), BF16), jax.ShapeDtypeStruct((b, l, n_dt), F32)],
        compiler_params=_cparams(2),
        name="in_projection",
    )(x, mod, mod, gain, w_in_p)


def _ssd_kernel(*refs, nc, seq, has_h0, emit_state):
    it = iter(refs)
    x_ref, b_ref, c_ref, dt_ref = next(it), next(it), next(it), next(it)
    cwx_ref, cwb_ref, cwc_ref = next(it), next(it), next(it)
    cbx_ref, cbb_ref, cbc_ref = next(it), next(it), next(it)
    par_ref = next(it)
    h0_ref = next(it) if has_h0 else None
    y_ref = next(it)
    st_ref = next(it) if emit_state else None
    xt_s, yd_s, bc_s, dtv_s, arow_s, acol_s, h_s = (next(it) for _ in range(7))

    q = SSD_CHUNK
    hpg = HEADS_PER_GROUP
    hd = SSD_HEAD_DIM
    wx = hpg * hd
    r = SSD_ROWS
    pack = BF16_SUBLANES
    row = lax.broadcasted_iota(jnp.int32, (q, 1), 0)
    si = lax.broadcasted_iota(jnp.int32, (q, q), 0)
    ti = lax.broadcasted_iota(jnp.int32, (q, q), 1)
    le = si <= ti
    ge = si >= ti
    tri_le = jnp.where(le, 1.0, 0.0).astype(BF16)
    tri_ge = jnp.where(ge, 1.0, 0.0).astype(BF16)
    par = par_ref[0]
    bias_t, alog_t, d_t = par[0:r], par[r:2 * r], par[2 * r:3 * r]

    for d in range(2):
        if has_h0:
            h_s[d] = h0_ref[0, d].reshape(wx, SSD_STATE)
        else:
            h_s[d] = jnp.zeros((wx, SSD_STATE), F32)

    def conv_silu(ref, w_ref, bias_ref, c):
        off = pl.multiple_of(c * q, q)
        u = ref[0, pl.ds(off, q), :].astype(F32)
        before = ref[0, pl.ds(pl.multiple_of(jnp.maximum(off - pack, 0), pack), pack), :]
        after = ref[0, pl.ds(pl.multiple_of(jnp.minimum(off + q, seq - pack), pack), pack), :]
        prev = before[pack - 1:pack, :].astype(F32) * jnp.where(c > 0, 1.0, 0.0)
        nxt = after[0:1, :].astype(F32) * jnp.where(c < nc - 1, 1.0, 0.0)
        up = jnp.where(row == 0, prev, pltpu.roll(u, 1, 0))
        un = jnp.where(row == q - 1, nxt, pltpu.roll(u, q - 1, 0))
        w = w_ref[...]
        return _silu(up * w[0:1] + u * w[1:2] + un * w[2:3] + bias_ref[...])

    tri_rows = jnp.concatenate([tri_le, tri_ge], axis=1)
    tri_cols = jnp.concatenate([tri_ge, tri_le], axis=0)
    fwd_row = lax.broadcasted_iota(jnp.int32, (r, 1), 0) < hpg
    fwd_lane = lax.broadcasted_iota(jnp.int32, (1, r), 1) < hpg

    def prep_body(c, carry):
        off = pl.multiple_of(c * q, q)
        xt = conv_silu(x_ref, cwx_ref, cbx_ref, c).T
        xt_s[c] = xt
        bc_s[pl.ds(off, q), 0:SSD_STATE] = conv_silu(b_ref, cwb_ref, cbb_ref, c).astype(BF16)
        bc_s[pl.ds(off, q), SSD_STATE:2 * SSD_STATE] = conv_silu(c_ref, cwc_ref, cbc_ref, c).astype(BF16)
        dtr = dt_ref[0, pl.ds(off, q), :].T[0:r, :] + bias_t
        dtv = jnp.maximum(dtr, 0.0) + jnp.log1p(jnp.exp(-jnp.abs(dtr)))
        dtv_s[c] = dtv
        parts = _split3(dtv * (-jnp.exp(alog_t)))
        rows = sum(_dot(part, tri_rows) for part in parts)
        cols = sum(lax.dot_general(tri_cols, part, NT_DIMS, preferred_element_type=F32) for part in parts)
        arow_s[c] = jnp.where(fwd_row, rows[:, 0:q], rows[:, q:2 * q])
        acol_s[c] = jnp.where(fwd_lane, cols[0:q, :], cols[q:2 * q, :])
        return carry

    def chunk(c, d):
        off = pl.multiple_of(c * q, q)
        base, mask = (0, le) if d == 0 else (hpg, ge)
        xt = xt_s[c]
        b16 = bc_s[pl.ds(off, q), 0:SSD_STATE]
        c16 = bc_s[pl.ds(off, q), SSD_STATE:2 * SSD_STATE]
        dtv, a_rows, a_cols = dtv_s[c], arow_s[c], acol_s[c]
        tot = a_rows[:, q - 1:q] if d == 0 else a_rows[:, 0:1]
        st = lax.dot_general(b16, c16, NT_DIMS, preferred_element_type=F32)
        hs = h_s[d]
        yoff = lax.dot_general(hs.astype(BF16), c16, NT_DIMS, preferred_element_type=F32)
        yield
        outs, xdecs, hnew = [], [], []
        for hl in range(hpg):
            k = base + hl
            rs = slice(hl * hd, (hl + 1) * hd)
            a_t = a_rows[k:k + 1, :]
            tot_k = tot[k:k + 1, :]
            seg = a_t - a_cols[:, k:k + 1]
            w = (st * jnp.exp(jnp.where(mask, seg, -jnp.inf))).astype(BF16)
            xdt = xt[rs, :] * dtv[k:k + 1, :]
            outs.append(_dot(xdt.astype(BF16), w) + yoff[rs, :] * jnp.exp(a_t))
            xdecs.append((xdt * jnp.exp(tot_k - a_t)).astype(BF16))
            hnew.append(hs[rs, :] * jnp.exp(tot_k))
            yield
        h_s[d] = jnp.concatenate(hnew, axis=0) + _dot(jnp.concatenate(xdecs, axis=0), b16)
        yd_s[d, c] = jnp.concatenate(outs, axis=0)

    def scan_body(i, carry):
        for _ in itertools.zip_longest(chunk(i, 0), chunk(nc - 1 - i, 1)):
            pass
        return carry

    def out_body(c, carry):
        xt = xt_s[c]
        skip = [xt[hl * hd:(hl + 1) * hd, :] * d_t[hl:hl + 1, :] for hl in range(hpg)]
        yt = yd_s[0, c] + yd_s[1, c] + jnp.concatenate(skip, axis=0)
        y_ref[0, pl.ds(pl.multiple_of(c * q, q), q), :] = yt.T
        return carry

    lax.fori_loop(0, nc, prep_body, 0, unroll=4)
    lax.fori_loop(0, nc, scan_body, 0, unroll=2)
    lax.fori_loop(0, nc, out_body, 0, unroll=4)
    if emit_state:
        for d in range(2):
            st_ref[0, d] = h_s[d].reshape(hpg, SSD_HEAD_DIM, SSD_STATE)


def _ssd_mixer(proj, dt, conv_w, conv_b, par, h0, emit_state):
    b, l, _ = proj.shape
    g = SSD_GROUPS
    wx = HEADS_PER_GROUP * SSD_HEAD_DIM
    n = SSD_STATE
    nc = l // SSD_CHUNK
    has_h0 = h0 is not None

    def col(width, start):
        blk = start // width
        return pl.BlockSpec((1, l, width), lambda i, j: (i, 0, blk + j))

    def cw(width, start, rows):
        blk = start // width
        return pl.BlockSpec((rows, width), lambda i, j: (0, blk + j))

    in_specs = [
        col(wx, COL_X), col(n, COL_B), col(n, COL_C), col(LANES, 0),
        cw(wx, 0, 3), cw(n, SSD_INNER, 3), cw(n, SSD_INNER + g * n, 3),
        cw(wx, 0, 1), cw(n, SSD_INNER, 1), cw(n, SSD_INNER + g * n, 1),
        pl.BlockSpec((1, 3 * SSD_ROWS, LANES), lambda i, j: (j, 0, 0)),
    ]
    args = [proj, proj, proj, dt, conv_w, conv_w, conv_w, conv_b, conv_b, conv_b, par]
    state_spec = pl.BlockSpec((1, 2, HEADS_PER_GROUP, SSD_HEAD_DIM, n), lambda i, j: (i, 0, j, 0, 0))
    if has_h0:
        in_specs.append(state_spec)
        args.append(h0)
    out_specs = [pl.BlockSpec((1, l, wx), lambda i, j: (i, 0, j))]
    out_shape = [jax.ShapeDtypeStruct((b, l, SSD_INNER), F32)]
    if emit_state:
        out_specs.append(state_spec)
        out_shape.append(jax.ShapeDtypeStruct((b, 2, SSD_HEADS, SSD_HEAD_DIM, n), F32))
    outs = pl.pallas_call(
        functools.partial(_ssd_kernel, nc=nc, seq=l, has_h0=has_h0, emit_state=emit_state),
        grid=(b, g),
        in_specs=in_specs,
        out_specs=out_specs,
        out_shape=out_shape,
        scratch_shapes=[
            pltpu.VMEM((nc, wx, SSD_CHUNK), F32),
            pltpu.VMEM((2, nc, wx, SSD_CHUNK), F32),
            pltpu.VMEM((l, 2 * n), BF16),
            pltpu.VMEM((nc, SSD_ROWS, SSD_CHUNK), F32),
            pltpu.VMEM((nc, SSD_ROWS, SSD_CHUNK), F32),
            pltpu.VMEM((nc, SSD_CHUNK, SSD_ROWS), F32),
            pltpu.VMEM((2, wx, n), F32),
        ],
        compiler_params=_cparams(2),
        name="ssd_mixer",
    )(*args)
    return (outs[0], outs[1]) if emit_state else (outs[0], None)


def _rope(x, cos, sin_signed, pair):
    n = x.shape[1]
    lane = lax.broadcasted_iota(jnp.int32, (1, n), 1)
    first = (lane & (2 * pair - 1)) < pair
    partner = jnp.where(first, pltpu.roll(x, n - pair, 1), pltpu.roll(x, pair, 1))
    return x * cos + partner * sin_signed


def _tile_lanes(x, k):
    return jnp.concatenate([x] * k, axis=1)


def _qkv_kernel(*refs, nct, has_pos, emit_cache):
    it = iter(refs)
    mla_ref, gqa_ref = next(it), next(it)
    qg_ref, kvg_ref, wuq_ref, wk_ref, wv_ref, gq_ref, gk_ref = (next(it) for _ in range(7))
    if has_pos:
        mcos_ref, msin_ref, gcos_ref, gsin_ref = (next(it) for _ in range(4))
    if nct:
        cckv_ref, ckr_ref, cgk_ref, cgv_ref = (next(it) for _ in range(4))
    qm_ref, km_ref, vm_ref, qg_out, kg_ref, vg_ref = (next(it) for _ in range(6))
    if emit_cache:
        ckv_out, kpe_out, kn_out, v_out = (next(it) for _ in range(4))

    t = pl.program_id(1)
    lane = lax.broadcasted_iota(jnp.int32, (1, LANES), 1)
    first_half = lane < GQA_HEAD_DIM
    mla_scale = math.log2(math.e) / math.sqrt(MLA_NOPE + MLA_ROPE)
    gqa_scale = math.log2(math.e) / math.sqrt(GQA_HEAD_DIM)
    ones_lane = jnp.where(lane == GQA_HEAD_DIM, 1.0, 0.0)

    def write_mla_kv(ckv, kpe_r):
        c16 = ckv.astype(BF16)
        km_ref[0] = (_dot(c16, wk_ref[...]) + _tile_lanes(kpe_r, MLA_HEADS)).astype(BF16)
        vm_ref[0] = (_dot(c16, wv_ref[...]) + _tile_lanes(ones_lane, MLA_HEADS)).T.astype(BF16)

    def new_tile():
        m = mla_ref[0].astype(F32)
        q_lat = m[:, 0:MLA_Q_LORA]
        kv_lat = m[:, MLA_Q_LORA:MLA_Q_LORA + MLA_KV_LORA]
        kpe = m[:, MLA_Q_LORA + MLA_KV_LORA:]
        qn = _rms(q_lat) * qg_ref[...]
        qm = _dot(qn.astype(BF16), wuq_ref[...])
        ckv = _rms(kv_lat) * kvg_ref[...]
        kpe_r = kpe
        if has_pos:
            mcos, msin = mcos_ref[...], msin_ref[...]
            qm = _rope(qm, _tile_lanes(mcos, MLA_HEADS), _tile_lanes(msin, MLA_HEADS), MLA_ROPE // 4)
            kpe_r = _rope(kpe, mcos, msin, MLA_ROPE // 4)
        qm_ref[0] = (qm * mla_scale).astype(BF16)
        write_mla_kv(ckv, kpe_r)

        g = gqa_ref[0].astype(F32)
        if has_pos:
            gcos, gsin = gcos_ref[...], gsin_ref[...]
        for h in range(GQA_HEADS):
            xh = g[:, h * LANES:(h + 1) * LANES]
            qh = _rms(xh, GQA_HEAD_DIM) * gq_ref[...]
            if has_pos:
                qh = _rope(qh, gcos, gsin, GQA_HEAD_DIM // 4)
            qg_out[0, :, h * LANES:(h + 1) * LANES] = (qh * gqa_scale).astype(BF16)
        kns = []
        for j in range(GQA_KV_HEADS):
            c0 = GQA_HEADS * LANES + j * LANES
            kn = _rms(g[:, c0:c0 + LANES]) * gk_ref[...]
            kns.append(kn)
            kr = _rope(kn, gcos, gsin, GQA_HEAD_DIM // 4) if has_pos else kn
            kg_ref[0, :, j * LANES:(j + 1) * LANES] = kr.astype(BF16)
        v0 = (GQA_HEADS + GQA_KV_HEADS) * LANES
        v = g[:, v0:v0 + GQA_KV_HEADS * LANES]
        for j in range(GQA_KV_HEADS):
            vj = jnp.where(first_half, v[:, j * LANES:(j + 1) * LANES], ones_lane)
            vg_ref[0, j * LANES:(j + 1) * LANES, :] = vj.T.astype(BF16)
        if emit_cache:
            ckv_out[0] = ckv
            kpe_out[0] = kpe[:, MLA_NOPE:MLA_NOPE + MLA_ROPE]
            kn_out[0] = jnp.where(first_half, kns[0], kns[1])
            v_out[0] = jnp.where(first_half, v[:, 0:LANES], v[:, LANES:2 * LANES])

    def ctx_tile():
        write_mla_kv(cckv_ref[0], ckr_ref[0])
        k = cgk_ref[0]
        k_sw = pltpu.roll(k, GQA_HEAD_DIM, 1)
        kg_ref[0, :, 0:LANES] = jnp.where(first_half, k, k_sw).astype(BF16)
        kg_ref[0, :, LANES:2 * LANES] = jnp.where(first_half, k_sw, k).astype(BF16)
        v = cgv_ref[0]
        vg_ref[0, 0:LANES, :] = jnp.where(first_half, v, ones_lane).T.astype(BF16)
        vg_ref[0, LANES:2 * LANES, :] = jnp.where(first_half, pltpu.roll(v, GQA_HEAD_DIM, 1), ones_lane).T.astype(BF16)

    if nct:
        pl.when(t < nct)(ctx_tile)
        pl.when(t >= nct)(new_tile)
    else:
        new_tile()


def _attention_operands(proj, wts, tables, ctx, layer, emit_cache, tm):
    b, l, _ = proj.shape
    has_pos = tables is not None
    nct = 0 if ctx is None else ctx[0].shape[2] // tm
    lk = l + nct * tm
    nt = lk // tm

    def new_t(t):
        return jnp.maximum(t - nct, 0) if nct else t

    in_specs = [
        pl.BlockSpec((1, tm, 512), lambda i, t: (i, new_t(t), COL_MLA // 512)),
        pl.BlockSpec((1, tm, 1024), lambda i, t: (i, new_t(t), COL_GQA // 1024)),
    ] + [_const_spec(w.shape) for w in wts]
    args = [proj, proj] + list(wts)
    if has_pos:
        in_specs += [pl.BlockSpec((tm, LANES), lambda i, t: (new_t(t), 0))] * 4
        args += list(tables)
    if nct:
        in_specs += [pl.BlockSpec((1, None, tm, LANES), lambda i, t: (i, layer, jnp.minimum(t, nct - 1), 0))] * 4
        args += list(ctx)
    q_spec = lambda w: pl.BlockSpec((1, tm, w), lambda i, t: (i, new_t(t), 0))
    k_spec = lambda w: pl.BlockSpec((1, tm, w), lambda i, t: (i, t, 0))
    vt_spec = lambda w: pl.BlockSpec((1, w, tm), lambda i, t: (i, 0, t))
    out_specs = [q_spec(512), k_spec(512), vt_spec(512), q_spec(512), k_spec(256), vt_spec(256)]
    out_shape = [
        jax.ShapeDtypeStruct((b, l, 512), BF16), jax.ShapeDtypeStruct((b, lk, 512), BF16),
        jax.ShapeDtypeStruct((b, 512, lk), BF16), jax.ShapeDtypeStruct((b, l, 512), BF16),
        jax.ShapeDtypeStruct((b, lk, 256), BF16), jax.ShapeDtypeStruct((b, 256, lk), BF16),
    ]
    if emit_cache:
        out_specs += [q_spec(MLA_KV_LORA), q_spec(MLA_ROPE), q_spec(LANES), q_spec(LANES)]
        out_shape += [
            jax.ShapeDtypeStruct((b, l, MLA_KV_LORA), F32), jax.ShapeDtypeStruct((b, l, MLA_ROPE), F32),
            jax.ShapeDtypeStruct((b, l, LANES), F32), jax.ShapeDtypeStruct((b, l, LANES), F32),
        ]
    return pl.pallas_call(
        functools.partial(_qkv_kernel, nct=nct, has_pos=has_pos, emit_cache=emit_cache),
        grid=(b, nt),
        in_specs=in_specs,
        out_specs=out_specs,
        out_shape=out_shape,
        compiler_params=_cparams(2),
        name="attention_operands",
    )(*args)


def _attn_kernel(q_ref, k_ref, vt_ref, o_ref, *, shared_kv):
    lane = lax.broadcasted_iota(jnp.int32, (1, LANES), 1)
    heads = (0, 1)
    cols = [slice(0, LANES) if shared_kv else slice(a * LANES, (a + 1) * LANES) for a in heads]
    tsub = min(ATTN_SUBTILE, q_ref.shape[1])
    units = [(i, r) for i in range(q_ref.shape[0]) for r in range(0, q_ref.shape[1], tsub)]

    def scores(u):
        i, r = u
        return [lax.dot_general(k_ref[i, :, cols[a]], q_ref[i, r:r + tsub, a * LANES:(a + 1) * LANES], NT_DIMS,
                                preferred_element_type=F32) for a in heads]

    def finish(u, sts):
        i, r = u
        pts = []
        for st in sts:
            part = jnp.max(st.reshape(-1, 2 * LANES, st.shape[1]), axis=0)
            pts.append(jnp.exp2(st - jnp.max(part, axis=0, keepdims=True)).astype(BF16))
        accs = [_dot(vt_ref[i, cols[a], :], pts[a]) for a in heads]
        outs = [(acc / acc[GQA_HEAD_DIM:GQA_HEAD_DIM + 1, :]).T for acc in accs]
        o_ref[i, r:r + tsub, :] = jnp.where(
            lane < GQA_HEAD_DIM, outs[0], pltpu.roll(outs[1], GQA_HEAD_DIM, 1)).astype(BF16)

    sts = scores(units[0])
    for n, u in enumerate(units):
        nxt = scores(units[n + 1]) if n + 1 < len(units) else None
        finish(u, sts)
        sts = nxt


def _attention(q, k, v, shared_kv, tq, nb):
    b, l, _ = q.shape
    lk = k.shape[1]
    kw = LANES if shared_kv else 2 * LANES
    return pl.pallas_call(
        functools.partial(_attn_kernel, shared_kv=shared_kv),
        grid=(b // nb, 2, l // tq),
        in_specs=[
            pl.BlockSpec((nb, tq, 2 * LANES), lambda i, j, t: (i, t, j)),
            pl.BlockSpec((nb, lk, kw), lambda i, j, t: (i, 0, j)),
            pl.BlockSpec((nb, kw, lk), lambda i, j, t: (i, j, 0)),
        ],
        out_specs=pl.BlockSpec((nb, tq, LANES), lambda i, j, t: (i, t, j)),
        out_shape=jax.ShapeDtypeStruct((b, l, 2 * LANES), BF16),
        compiler_params=_cparams(3),
        name="attention_shared_kv" if shared_kv else "attention",
    )(q, k, v)


def _post_kernel(*refs, moe):
    it = iter(refs)
    ys_ref, z_ref, ym_ref, yg_ref, x_ref, g1_ref, sh2_ref, sc2_ref = (next(it) for _ in range(8))
    ng_ref, gpost_ref, gpre_ref, wout_ref = (next(it) for _ in range(4))
    wr_ref = next(it) if moe else None
    x1_ref, h2_ref = next(it), next(it)
    comb_ref, combt_ref = (next(it), next(it)) if moe else (None, None)

    tm = x_ref.shape[1]
    if moe:
        w_hi = wr_ref[...].astype(BF16)
        w_lo = (wr_ref[...] - w_hi.astype(F32)).astype(BF16)
        lane = lax.broadcasted_iota(jnp.int32, (1, LANES), 1).astype(F32)

    def rows(r0, r1):
        y = _rms(ys_ref[0, r0:r1, :] * _silu(z_ref[0, r0:r1, :].astype(F32))) * ng_ref[...]
        cat = jnp.concatenate([y.astype(BF16), ym_ref[0, r0:r1, :], yg_ref[0, r0:r1, :]], axis=1)
        yield
        mix = _dot(cat, wout_ref[...])
        yield
        x1 = x_ref[0, r0:r1, :] + g1_ref[0] * (_rms(mix) * gpost_ref[...])
        x1_ref[0, r0:r1, :] = x1
        h2 = _rms(x1) * gpre_ref[...] * (1.0 + sc2_ref[0]) + sh2_ref[0]
        h16 = h2.astype(BF16)
        h2_ref[0, r0:r1, :] = h16
        if moe:
            yield
            logits = _dot(h16, w_hi) + _dot(h16, w_lo)
            yield
            lg = jnp.where(lane < N_EXPERTS, logits, -jnp.inf)
            m1 = jnp.max(lg, axis=-1, keepdims=True)
            i1 = jnp.min(jnp.where(lg == m1, lane, float(LANES)), axis=-1, keepdims=True)
            lg2 = jnp.where(lane == i1, -jnp.inf, lg)
            m2 = jnp.max(lg2, axis=-1, keepdims=True)
            i2 = jnp.min(jnp.where(lg2 == m2, lane, float(LANES)), axis=-1, keepdims=True)
            e = jnp.exp(m2 - m1)
            comb = jnp.where(lane == i1, 1.0 / (1.0 + e), jnp.where(lane == i2, e / (1.0 + e), 0.0))
            comb_ref[0, r0:r1, :] = comb
            combt_ref[:, r0:r1] = comb.T[0:N_EXPERTS, :]

    half = tm // 2 if tm % 256 == 0 else tm
    pending = [rows(r, r + half) for r in range(0, tm, half)]
    live = []
    while pending or live:
        if pending:
            live.append(pending.pop(0))
        for g in list(live):
            if next(g, StopIteration) is StopIteration:
                live.remove(g)


def _post_attention(yssd, proj, ym, yg, x, mod, mod_row, ng, gpost, gpre, w_out, w_router, tm):
    b, l, d = x.shape
    moe = w_router is not None
    tok = lambda w: pl.BlockSpec((1, tm, w), lambda i, t: (i, t, 0))
    modk = lambda k: pl.BlockSpec((1, 1, d), lambda i, t: (mod_row(i), 0, k))
    in_specs = [tok(SSD_INNER), tok(SSD_INNER), tok(256), tok(256), tok(d), modk(2), modk(3), modk(4),
                _const_spec(ng.shape), _const_spec(gpost.shape), _const_spec(gpre.shape),
                _const_spec(w_out.shape)]
    args = [yssd, proj, ym, yg, x, mod, mod, mod, ng, gpost, gpre, w_out]
    out_specs = [tok(d), tok(d)]
    out_shape = [jax.ShapeDtypeStruct((b, l, d), F32), jax.ShapeDtypeStruct((b, l, d), BF16)]
    if moe:
        in_specs.append(_const_spec(w_router.shape))
        args.append(w_router)
        out_specs += [tok(LANES), pl.BlockSpec((N_EXPERTS, tm), lambda i, t: (0, i * (l // tm) + t))]
        out_shape += [jax.ShapeDtypeStruct((b, l, LANES), F32), jax.ShapeDtypeStruct((N_EXPERTS, b * l), F32)]
    outs = pl.pallas_call(
        functools.partial(_post_kernel, moe=moe),
        grid=(b, l // tm),
        in_specs=in_specs,
        out_specs=out_specs,
        out_shape=out_shape,
        compiler_params=_cparams(2),
        name="post_attention",
    )(*args)
    return outs if moe else (outs[0], outs[1], None, None)


def _ffn_kernel(h_ref, x1_ref, g2_ref, gp_ref, wg_ref, wu_ref, wd_ref, o_ref):
    h = h_ref[0]
    act = (_silu(_dot(h, wg_ref[...])) * _dot(h, wu_ref[...])).astype(BF16)
    f = _dot(act, wd_ref[...])
    o_ref[0] = x1_ref[0] + g2_ref[0] * (_rms(f) * gp_ref[...])


def _dense_ffn(h2, x1, mod, mod_row, gp, wg, wu, wd, tm):
    b, l, d = x1.shape
    tok = pl.BlockSpec((1, tm, d), lambda i, t: (i, t, 0))
    return pl.pallas_call(
        _ffn_kernel,
        grid=(b, l // tm),
        in_specs=[tok, tok, pl.BlockSpec((1, 1, d), lambda i, t: (mod_row(i), 0, 5)),
                  _const_spec(gp.shape), _const_spec(wg.shape), _const_spec(wu.shape), _const_spec(wd.shape)],
        out_specs=tok,
        out_shape=jax.ShapeDtypeStruct((b, l, d), F32),
        compiler_params=_cparams(2),
        name="dense_ffn",
    )(h2, x1, mod, gp, wg, wu, wd)


ATTN_TILE = 2048
ATTN_SUBTILE = 256

MOE_BLOCK = 1024
MOE_CAP = 256
MOE_SUPER = 2048


def _moe_kernel(h_ref, comb_ref, combt_ref, x1_ref, g2_ref, gp_ref, wg_ref, wu_ref, wd_ref, o_ref, pos_s,
                *, tb, cap):
    e = pl.program_id(1)
    nsub = o_ref.shape[0] // tb

    @pl.when(e == 0)
    def _():
        o_ref[...] = jnp.zeros(o_ref.shape, F32)
        si = lax.broadcasted_iota(jnp.int32, (tb, tb), 0)
        ti = lax.broadcasted_iota(jnp.int32, (tb, tb), 1)
        before = jnp.where(si < ti, 1.0, 0.0).astype(BF16)
        for u in range(nsub):
            sel = jnp.where(combt_ref[:, u * tb:(u + 1) * tb] > 0.0, 1.0, 0.0).astype(BF16)
            pos_s[:, u * tb:(u + 1) * tb] = _dot(sel, before)

    lane = lax.broadcasted_iota(jnp.int32, (1, LANES), 1)
    half = cap // 2

    def block(u, carry):
        off = pl.multiple_of(u * tb, tb)
        sel_e = combt_ref[pl.ds(e, 1), pl.ds(off, tb)] > 0.0
        pos_e = pos_s[pl.ds(e, 1), pl.ds(off, tb)]
        n_e = jnp.sum(jnp.where(sel_e, 1.0, 0.0)).astype(jnp.int32)
        gate = jnp.sum(jnp.where(lane == e, comb_ref[pl.ds(off, tb), :], 0.0), axis=-1, keepdims=True)

        def tile(first, rows):
            slot = lax.broadcasted_iota(jnp.int32, (rows, 1), 0).astype(F32)
            pick = jnp.where(sel_e & (pos_e - first == slot), 1.0, 0.0).astype(BF16)
            xg = _dot(pick, h_ref[pl.ds(off, tb), :]).astype(BF16)
            act = (_silu(_dot(xg, wg_ref[0])) * _dot(xg, wu_ref[0])).astype(BF16)
            y = _dot(act, wd_ref[0]).astype(BF16)
            o_ref[pl.ds(off, tb), :] += gate * lax.dot_general(pick, y, TN_DIMS, preferred_element_type=F32)

        def full_tile(j, carry):
            tile(j * float(cap), cap)
            return carry

        n_full = (n_e + half - 1) // cap
        lax.fori_loop(0, n_full, full_tile, 0)
        pl.when(n_e > n_full * cap)(lambda: tile(n_full * float(cap), half))
        return carry

    lax.fori_loop(0, nsub, block, 0)

    @pl.when(e == pl.num_programs(1) - 1)
    def _():
        def residual(u, carry):
            rows = pl.ds(pl.multiple_of(u * tb, tb), tb)
            o_ref[rows, :] = x1_ref[rows, :] + g2_ref[0] * (_rms(o_ref[rows, :]) * gp_ref[...])
            return carry

        lax.fori_loop(0, nsub, residual, 0)


def _moe_ffn(h2, comb, combt, x1, mod, mod_row, gp, wg, wu, wd, tb, sup):
    t, d = h2.shape
    n_e, _, ff = wg.shape
    cap = -(-(MOE_CAP * tb // MOE_BLOCK) // 16) * 16
    once = pl.Buffered(1)
    return pl.pallas_call(
        functools.partial(_moe_kernel, tb=tb, cap=cap),
        grid=(t // sup, n_e),
        in_specs=[pl.BlockSpec((sup, d), lambda s, e: (s, 0), pipeline_mode=once),
                  pl.BlockSpec((sup, LANES), lambda s, e: (s, 0), pipeline_mode=once),
                  pl.BlockSpec((N_EXPERTS, sup), lambda s, e: (0, s), pipeline_mode=once),
                  pl.BlockSpec((sup, d), lambda s, e: (s, 0), pipeline_mode=once),
                  pl.BlockSpec((1, 1, d), lambda s, e: (mod_row(s), 0, 5)),
                  pl.BlockSpec((1, d), lambda s, e: (0, 0)),
                  pl.BlockSpec((1, d, ff), lambda s, e: (e, 0, 0)),
                  pl.BlockSpec((1, d, ff), lambda s, e: (e, 0, 0)),
                  pl.BlockSpec((1, ff, d), lambda s, e: (e, 0, 0))],
        out_specs=pl.BlockSpec((sup, d), lambda s, e: (s, 0), pipeline_mode=once),
        out_shape=jax.ShapeDtypeStruct((t, d), F32),
        scratch_shapes=[pltpu.VMEM((N_EXPERTS, sup), F32)],
        compiler_params=_cparams(2),
        name="moe_ffn",
    )(h2, comb, combt, x1, mod, gp, wg, wu, wd)


def _in_proj_columns():
    o_dt = 1536
    o_ql, o_kv, o_kpe = 1552, 1808, 1936
    o_gq, o_gk, o_gv = 1968, 2224, 2352
    idx = np.full((IN_COLS_PADDED,), -1, np.int64)
    idx[0:1536] = np.arange(1536)
    idx[COL_MLA:COL_MLA + 256] = o_ql + np.arange(256)
    idx[COL_MLA + 256:COL_MLA + 384] = o_kv + np.arange(128)
    idx[COL_MLA + 384 + MLA_NOPE:COL_MLA + 384 + MLA_NOPE + MLA_ROPE] = o_kpe + np.arange(MLA_ROPE)
    for h in range(GQA_HEADS):
        idx[COL_GQA + h * LANES:COL_GQA + h * LANES + 64] = o_gq + h * 64 + np.arange(64)
    for j in range(GQA_KV_HEADS):
        for r in range(2):
            c0 = COL_GQA + 512 + j * LANES + r * 64
            idx[c0:c0 + 64] = o_gk + j * 64 + np.arange(64)
            idx[c0 + 256:c0 + 256 + 64] = o_gv + j * 64 + np.arange(64)
    for g in range(SSD_GROUPS):
        for d in range(2):
            c0 = COL_DT + g * LANES + d * HEADS_PER_GROUP
            idx[c0:c0 + HEADS_PER_GROUP] = o_dt + d * SSD_HEADS + g * HEADS_PER_GROUP + np.arange(HEADS_PER_GROUP)
    return idx


def _gather_cols(w, idx):
    cols = jnp.take(w, jnp.asarray(np.maximum(idx, 0)), axis=1)
    return jnp.where(jnp.asarray(idx >= 0)[None, :], cols, 0.0)


def _mla_weight_columns():
    per_q = MLA_NOPE + MLA_ROPE
    uq = np.full((MLA_HEADS * LANES,), -1, np.int64)
    uk = np.full((MLA_HEADS * LANES,), -1, np.int64)
    uv = np.full((MLA_HEADS * LANES,), -1, np.int64)
    for h in range(MLA_HEADS):
        uq[h * LANES:h * LANES + per_q] = h * per_q + np.arange(per_q)
        uk[h * LANES:h * LANES + MLA_NOPE] = h * (MLA_NOPE + MLA_V) + np.arange(MLA_NOPE)
        uv[h * LANES:h * LANES + MLA_V] = h * (MLA_NOPE + MLA_V) + MLA_NOPE + np.arange(MLA_V)
    return uq, uk, uv


def _rope_tables(n_tokens):
    t = np.arange(n_tokens)
    row, colp = (t // GRID_W).astype(np.float64), (t % GRID_W).astype(np.float64)

    def table(width, lane0, head_dim, reps):
        cos = np.ones((n_tokens, width))
        sin = np.zeros((n_tokens, width))
        half = head_dim // 2
        quarter = half // 2
        for dim in range(head_dim):
            pos = row if dim < half else colp
            inv = ROPE_BASE ** (-(2.0 * (dim % quarter)) / half)
            ang = pos * inv
            sign = -1.0 if (dim % half) < quarter else 1.0
            for r in range(reps):
                cos[:, lane0 + r * head_dim + dim] = np.cos(ang)
                sin[:, lane0 + r * head_dim + dim] = sign * np.sin(ang)
        return jnp.asarray(cos, F32), jnp.asarray(sin, F32)

    mcos, msin = table(LANES, MLA_NOPE, MLA_ROPE, 1)
    gcos, gsin = table(LANES, 0, GQA_HEAD_DIM, 2)
    return mcos, msin, gcos, gsin


def _trunk_layer(i, x, mod_i, mod_row, p, tables, ctx):
    b, l, d = x.shape
    tm = min(512, l)
    emit = ctx is None
    proj, dt = _in_projection(x, mod_i, mod_row, p["g_mix_pre"][i], p["w_in"][i], tm)
    h0 = None if ctx is None else ctx["state"][:, i]
    yssd, state = _ssd_mixer(proj, dt, p["conv_w"][i], p["conv_b"][i], p["ssd_par"][i], h0, emit)
    ctx_kv = None if ctx is None else ctx["kv"]
    ops = _attention_operands(proj, p["attn_w"][i], tables, ctx_kv, i, emit, tm)
    qm, km, vm, qg, kg, vg = ops[:6]
    tq = min(ATTN_TILE, l)
    nb = math.gcd(b, max(1, 1024 // km.shape[1]))
    ym = _attention(qm, km, vm, False, tq, nb)
    yg = _attention(qg, kg, vg, True, tq, nb)
    moe = i % 2 == 1
    j = i // 2
    x1, h2, comb, combt = _post_attention(
        yssd, proj, ym, yg, x, mod_i, mod_row, p["ssd_norm_g"][i], p["g_mix_post"][i], p["g_ffn_pre"][i],
        p["w_out"][i], p["w_router"][j] if moe else None, tm)
    if moe:
        per_batch = ctx is not None
        sup = min(MOE_SUPER, l if per_batch else b * l)
        sup_row = (lambda s: mod_row(s // (l // sup))) if per_batch else mod_row
        x2 = _moe_ffn(h2.reshape(b * l, d), comb.reshape(b * l, LANES), combt, x1.reshape(b * l, d), mod_i,
                      sup_row, p["g_ffn_post"][i], p["moe_wg"][j], p["moe_wu"][j], p["moe_wd"][j],
                      min(MOE_BLOCK, sup), sup).reshape(b, l, d)
    else:
        x2 = _dense_ffn(h2, x1, mod_i, mod_row, p["g_ffn_post"][i],
                        p["ffn_wg"][j], p["ffn_wu"][j], p["ffn_wd"][j], tm)
    return x2, (state,) + tuple(ops[6:])


def kernel(x_prompt, x_sample, state_ssd, cache_mla_ckv, cache_mla_krope, cache_gqa_k, cache_gqa_v, c, c_ctx, w_mod, b_mod, g_mix_pre, g_mix_post, g_ffn_pre, g_ffn_post, w_in, ssd_conv_w, ssd_conv_b, ssd_A_log, ssd_dt_bias, ssd_D, ssd_norm_g, mla_q_norm_g, mla_w_uq, mla_kv_norm_g, mla_w_ukv, gqa_q_norm_g, gqa_k_norm_g, w_out, ffn_w_gate, ffn_w_up, ffn_w_down, moe_w_router, moe_w_gate, moe_w_up, moe_w_down):
    depth = w_in.shape[0]
    d = x_prompt.shape[-1]
    n_dec = x_sample.shape[0]
    hpg = HEADS_PER_GROUP

    in_idx = _in_proj_columns()
    uq_idx, uk_idx, uv_idx = _mla_weight_columns()
    row2 = lambda a: a.reshape(depth, 1, a.shape[-1])
    def ssd_rows(fwd, bwd):
        cols = [fwd.reshape(depth, SSD_GROUPS, hpg), bwd.reshape(depth, SSD_GROUPS, hpg),
                jnp.zeros((depth, SSD_GROUPS, SSD_ROWS - 2 * hpg), F32)]
        return jnp.concatenate(cols, axis=-1)

    par = jnp.concatenate([ssd_rows(ssd_dt_bias[:, 0], ssd_dt_bias[:, 1]),
                           ssd_rows(ssd_A_log[:, 0], ssd_A_log[:, 1]),
                           ssd_rows(ssd_D, jnp.zeros_like(ssd_D))], axis=-1)
    par = jnp.broadcast_to(par[..., None], par.shape + (LANES,))
    zeros64 = jnp.zeros((depth, 1, GQA_HEAD_DIM), F32)
    attn_w = []
    for i in range(depth):
        attn_w.append((
            mla_q_norm_g[i][None, :], mla_kv_norm_g[i][None, :],
            _gather_cols(mla_w_uq[i], uq_idx).astype(BF16),
            _gather_cols(mla_w_ukv[i], uk_idx).astype(BF16),
            _gather_cols(mla_w_ukv[i], uv_idx).astype(BF16),
            jnp.concatenate([gqa_q_norm_g[i][None, :], zeros64[i]], axis=1),
            jnp.concatenate([gqa_k_norm_g[i][None, :]] * 2, axis=1),
        ))
    p = dict(
        g_mix_pre=row2(g_mix_pre), g_mix_post=row2(g_mix_post), g_ffn_pre=row2(g_ffn_pre),
        g_ffn_post=row2(g_ffn_post), ssd_norm_g=row2(ssd_norm_g),
        w_in=[_gather_cols(w_in[i].astype(BF16), in_idx) for i in range(depth)],
        conv_w=ssd_conv_w, conv_b=row2(ssd_conv_b), ssd_par=par, attn_w=attn_w,
        w_out=w_out.astype(BF16),
        ffn_wg=ffn_w_gate.astype(BF16), ffn_wu=ffn_w_up.astype(BF16), ffn_wd=ffn_w_down.astype(BF16),
        w_router=jnp.pad(moe_w_router, ((0, 0), (0, 0), (0, LANES - N_EXPERTS))),
        moe_wg=moe_w_gate.astype(BF16), moe_wu=moe_w_up.astype(BF16), moe_wd=moe_w_down.astype(BF16),
    )

    rows = -(-(1 + n_dec) // 8) * 8
    c_all = jnp.concatenate([c_ctx[None, :], c, jnp.zeros((rows - 1 - n_dec, d), F32)], axis=0)
    mod = _modulation(c_all, w_mod, b_mod).reshape(depth, rows, 1, 6 * d)

    y = x_prompt
    collected = [[], [], [], [], []]
    for i in range(depth):
        y, outs = _trunk_layer(i, y, mod[i], lambda bi: 0, p, None, None)
        for lst, t in zip(collected, outs):
            lst.append(t)
    y_prompt = y
    bsz, seq = x_prompt.shape[:2]
    new_state = jnp.stack(collected[0], axis=1)
    new_ckv = jnp.stack(collected[1], axis=1)
    new_krope = jnp.stack(collected[2], axis=1)
    new_k = jnp.stack(collected[3], axis=1).reshape(bsz, depth, seq, GQA_KV_HEADS, GQA_HEAD_DIM)
    new_v = jnp.stack(collected[4], axis=1).reshape(bsz, depth, seq, GQA_KV_HEADS, GQA_HEAD_DIM)

    past = cache_mla_ckv.shape[2]
    krope_blk = jnp.pad(cache_mla_krope, ((0, 0), (0, 0), (0, 0), (MLA_NOPE, LANES - MLA_NOPE - MLA_ROPE)))
    ctx = dict(
        state=state_ssd,
        kv=(cache_mla_ckv, krope_blk,
            cache_gqa_k.reshape(n_dec, depth, past, LANES), cache_gqa_v.reshape(n_dec, depth, past, LANES)),
    )
    tables = _rope_tables(x_sample.shape[1])
    y = x_sample
    for i in range(depth):
        y, _ = _trunk_layer(i, y, mod[i], lambda bi: bi + 1, p, tables, ctx)
    return (y_prompt, y, new_state, new_ckv, new_krope, new_k, new_v)
```
